```python
import jax, jax.numpy as jnp
from jax import lax
import numpy as np

D_MODEL = 2048
BATCH = 2
SEQ = 4096
DEPTH = 4

HEAD_DIM = 64
D_MIX = D_MODEL
GMLP_WIDTH = D_MIX // 4
ATTN_WIDTH = D_MIX // 2
FNET_WIDTH = D_MIX // 4
GMLP_HEADS = GMLP_WIDTH // HEAD_DIM
ATTN_HEADS = ATTN_WIDTH // HEAD_DIM
FNET_GROUPS = FNET_WIDTH // HEAD_DIM
CHUNK = 128
Q_BLOCK = 128
DILATED_PATTERNS = ((128, 1), (512, 4), (2048, 16))
REL_BUCKETS = 32
REL_MAX_DISTANCE = 1024
D_FF = 5632
CONV_WIDTH = 3
EPS = 1e-6
IN_WIDTH = 2 * GMLP_WIDTH + 3 * ATTN_WIDTH + FNET_WIDTH

kernel_name = "hybrid_gmlp_dilated_fnet_encoder"


def _rms_norm(x, g):
    xf = x.astype(jnp.float32)
    y = xf * lax.rsqrt(jnp.mean(xf * xf, axis=-1, keepdims=True) + EPS)
    return (y * g.astype(jnp.float32)).astype(x.dtype)


def _t5_bucket(rel):
    half = REL_BUCKETS // 2
    max_exact = half // 2
    n = np.abs(rel)
    nl = np.maximum(n, max_exact).astype(np.float32)
    large = max_exact + (np.log(nl / max_exact) / np.log(REL_MAX_DISTANCE / max_exact)
                         * (half - max_exact)).astype(np.int32)
    large = np.minimum(large, half - 1)
    b = np.where(n < max_exact, n, large) + (rel > 0).astype(np.int32) * half
    return b.astype(np.int32)


def _dilated_attention(q, k, v, rel_bias):
    bsz, seq, heads, hd = q.shape
    scale = hd ** -0.5
    patterns = []
    for window, dil in DILATED_PATTERNS:
        half = window // (2 * dil)
        offs = dil * np.arange(-half, half + 1, dtype=np.int32)
        bias = rel_bias[jnp.asarray(_t5_bucket(offs))].T.astype(jnp.float32)
        patterns.append((jnp.asarray(offs), bias))

    def block(i):
        start = i * Q_BLOCK
        qb = lax.dynamic_slice_in_dim(q, start, Q_BLOCK, axis=1)
        qpos = start + jnp.arange(Q_BLOCK, dtype=jnp.int32)
        outs, lses = [], []
        for offs, bias in patterns:
            idx = qpos[:, None] + offs[None, :]
            valid = (idx >= 0) & (idx < seq)
            idx = jnp.clip(idx, 0, seq - 1)
            kg = jnp.take(k, idx, axis=1)
            vg = jnp.take(v, idx, axis=1)
            logits = jnp.einsum('bqhd,bqkhd->bqhk', qb, kg).astype(jnp.float32) * scale
            logits = logits + bias[None, None]
            logits = jnp.where(valid[None, :, None, :], logits, -1e30)
            m = jnp.max(logits, axis=-1, keepdims=True)
            p = jnp.exp(logits - m)
            s = jnp.sum(p, axis=-1, keepdims=True)
            o = jnp.einsum('bqhk,bqkhd->bqhd', (p / s).astype(v.dtype), vg)
            outs.append(o.astype(jnp.float32))
            lses.append((m + jnp.log(s))[..., 0])
        w = jax.nn.softmax(jnp.stack(lses, axis=0), axis=0)
        out = jnp.sum(w[..., None] * jnp.stack(outs, axis=0), axis=0)
        return out.astype(q.dtype)

    ob = lax.map(block, jnp.arange(seq // Q_BLOCK))
    return jnp.transpose(ob, (1, 0, 2, 3, 4)).reshape(bsz, seq, heads * hd)


def _hybrid_layer(x, w_in, gmlp_ws, gmlp_b, fnet_w, mix_gain, w_out, norm_mix,
                  norm_ffn, ffn_up, ffn_conv_w, ffn_conv_b, ffn_down, rel_bias):
    bsz, seq, _ = x.shape
    xn = _rms_norm(x, norm_mix)
    z = xn @ w_in
    o1 = 2 * GMLP_WIDTH
    o2 = o1 + 3 * ATTN_WIDTH
    za, zq, zc = z[..., :o1], z[..., o1:o2], z[..., o2:]

    za = jax.nn.gelu(za, approximate=False)
    u, vg = za[..., :GMLP_WIDTH], za[..., GMLP_WIDTH:]
    vg = vg.reshape(bsz, seq // CHUNK, CHUNK, GMLP_HEADS, HEAD_DIM)
    gate = jnp.einsum('hij,bcjhd->bcihd', gmlp_ws, vg) + gmlp_b.T[None, None, :, :, None]
    a_out = u * gate.reshape(bsz, seq, GMLP_WIDTH)

    q = zq[..., :ATTN_WIDTH].reshape(bsz, seq, ATTN_HEADS, HEAD_DIM)
    k = zq[..., ATTN_WIDTH:2 * ATTN_WIDTH].reshape(bsz, seq, ATTN_HEADS, HEAD_DIM)
    v = zq[..., 2 * ATTN_WIDTH:].reshape(bsz, seq, ATTN_HEADS, HEAD_DIM)
    b_out = _dilated_attention(q, k, v, rel_bias)

    zc = zc.reshape(bsz, seq, FNET_GROUPS, HEAD_DIM).astype(jnp.float32)
    f = jnp.fft.fft2(zc, axes=(1, 3), norm='ortho').real.astype(x.dtype)
    c_out = jnp.einsum('bsgc,gce->bsge', f, fnet_w).reshape(bsz, seq, FNET_WIDTH)

    ga = mix_gain[:GMLP_WIDTH]
    gb = mix_gain[GMLP_WIDTH:GMLP_WIDTH + ATTN_WIDTH]
    gc = mix_gain[GMLP_WIDTH + ATTN_WIDTH:]
    mixed = jnp.concatenate([_rms_norm(a_out, ga), _rms_norm(b_out, gb), _rms_norm(c_out, gc)], axis=-1)
    x = x + mixed @ w_out

    hn = _rms_norm(x, norm_ffn)
    h = hn @ ffn_up
    hp = jnp.pad(h, ((0, 0), (1, 1), (0, 0)))
    h = hp[:, :-2] * ffn_conv_w[0] + hp[:, 1:-1] * ffn_conv_w[1] + hp[:, 2:] * ffn_conv_w[2] + ffn_conv_b
    g, up = h[..., :D_FF], h[..., D_FF:]
    x = x + (jax.nn.silu(g) * up) @ ffn_down
    return x


def setup_inputs(seed: int = 0) -> dict:
    key = jax.random.key(seed)
    ks = jax.random.split(key, 16)
    f32 = jnp.float32
    nrm = lambda k, shape, s: jax.random.normal(k, shape, f32) * s
    return {
        'x': nrm(ks[0], (BATCH, SEQ, D_MODEL), 1.0),
        'w_in': nrm(ks[1], (DEPTH, D_MODEL, IN_WIDTH), D_MODEL ** -0.5),
        'gmlp_ws': nrm(ks[2], (DEPTH, GMLP_HEADS, CHUNK, CHUNK), CHUNK ** -0.5),
        'gmlp_b': 1.0 + nrm(ks[3], (DEPTH, GMLP_HEADS, CHUNK), 0.01),
        'fnet_w': nrm(ks[4], (DEPTH, FNET_GROUPS, HEAD_DIM, HEAD_DIM), HEAD_DIM ** -0.5),
        'mix_gain': 1.0 + nrm(ks[5], (DEPTH, D_MIX), 0.01),
        'w_out': nrm(ks[6], (DEPTH, D_MIX, D_MODEL), D_MIX ** -0.5),
        'norm_mix': 1.0 + nrm(ks[7], (DEPTH, D_MODEL), 0.01),
        'norm_ffn': 1.0 + nrm(ks[8], (DEPTH, D_MODEL), 0.01),
        'ffn_up': nrm(ks[9], (DEPTH, D_MODEL, 2 * D_FF), D_MODEL ** -0.5),
        'ffn_conv_w': nrm(ks[10], (DEPTH, CONV_WIDTH, 2 * D_FF), CONV_WIDTH ** -0.5),
        'ffn_conv_b': nrm(ks[11], (DEPTH, 2 * D_FF), 0.01),
        'ffn_down': nrm(ks[12], (DEPTH, D_FF, D_MODEL), D_FF ** -0.5),
        'rel_bias': nrm(ks[13], (REL_BUCKETS, ATTN_HEADS), 0.5),
        'final_norm': 1.0 + nrm(ks[14], (D_MODEL,), 0.01),
    }


def reference(x, w_in, gmlp_ws, gmlp_b, fnet_w, mix_gain, w_out, norm_mix, norm_ffn,
              ffn_up, ffn_conv_w, ffn_conv_b, ffn_down, rel_bias, final_norm):
    for l in range(DEPTH):
        x = _hybrid_layer(x, w_in[l], gmlp_ws[l], gmlp_b[l], fnet_w[l], mix_gain[l], w_out[l],
                          norm_mix[l], norm_ffn[l], ffn_up[l], ffn_conv_w[l], ffn_conv_b[l],
                          ffn_down[l], rel_bias)
    return _rms_norm(x, final_norm)
```

```python
import functools

import numpy as np
import jax
import jax.numpy as jnp
from jax import lax
from jax.experimental import pallas as pl
from jax.experimental.pallas import tpu as pltpu

D_MODEL = 2048
HEAD_DIM = 64
GMLP_WIDTH = 512
ATTN_WIDTH = 1024
FNET_WIDTH = 512
GMLP_HEADS = GMLP_WIDTH // HEAD_DIM
ATTN_HEADS = ATTN_WIDTH // HEAD_DIM
FNET_GROUPS = FNET_WIDTH // HEAD_DIM
CHUNK = 128
DILATIONS = (1, 4, 16)
REL_BUCKETS = 32
REL_MAX_DISTANCE = 1024
D_FF = 5632
EPS = 1e-6
IN_WIDTH = 2 * GMLP_WIDTH + 3 * ATTN_WIDTH + FNET_WIDTH
NEG_INF = -1e30

Q_OFF = 2 * GMLP_WIDTH
K_OFF = Q_OFF + ATTN_WIDTH
V_OFF = K_OFF + ATTN_WIDTH
C_OFF = V_OFF + ATTN_WIDTH

LANES = 128
BAND = 64
ATTN_QB = 128
ATTN_KB = ATTN_QB + 2 * BAND
FFT_N = 64
HALO = 8

F32 = jnp.float32
BF16 = jnp.bfloat16
MIB = 1024 * 1024


def _cparams(n_axes, vmem_mib):
    return pltpu.CompilerParams(dimension_semantics=("arbitrary",) * n_axes,
                                vmem_limit_bytes=vmem_mib * MIB)


def _rms(x, g):
    return x * lax.rsqrt(jnp.mean(x * x, axis=-1, keepdims=True) + EPS) * g


def _proj_in_kernel(x_ref, g_ref, w_ref, o_ref, xn_ref, *, gelu_tiles):
    j = pl.program_id(1)

    @pl.when(j == 0)
    def _():
        xn_ref[...] = _rms(x_ref[...], g_ref[...]).astype(BF16)

    acc = jnp.dot(xn_ref[...], w_ref[...], preferred_element_type=F32)

    @pl.when(j < gelu_tiles)
    def _():
        o_ref[...] = 0.5 * acc * (1.0 + lax.erf(acc * np.float32(2.0 ** -0.5)))

    @pl.when(j >= gelu_tiles)
    def _():
        o_ref[...] = acc


def _proj_in(x2, gain, w, tm=512, tn=512):
    t = x2.shape[0]
    return pl.pallas_call(
        functools.partial(_proj_in_kernel, gelu_tiles=Q_OFF // tn),
        grid=(t // tm, IN_WIDTH // tn),
        in_specs=[pl.BlockSpec((tm, D_MODEL), lambda i, j: (i, 0)),
                  pl.BlockSpec((1, D_MODEL), lambda i, j: (0, 0)),
                  pl.BlockSpec((D_MODEL, tn), lambda i, j: (0, j))],
        out_specs=pl.BlockSpec((tm, tn), lambda i, j: (i, j)),
        out_shape=jax.ShapeDtypeStruct((t, IN_WIDTH), F32),
        scratch_shapes=[pltpu.VMEM((tm, D_MODEL), BF16)],
        compiler_params=_cparams(2, 40),
        name="proj_in",
    )(x2, gain, w)


def _gmlp_kernel(u_ref, v_ref, ws_ref, b_ref, o_ref, *, chunks):
    lane = lax.broadcasted_iota(jnp.int32, (CHUNK, 2 * HEAD_DIM), 1)
    for c in range(chunks):
        rows = slice(c * CHUNK, (c + 1) * CHUNK)
        for p in range(GMLP_HEADS // 2):
            cols = slice(p * 2 * HEAD_DIM, (p + 1) * 2 * HEAD_DIM)
            both = jnp.dot(ws_ref[p], v_ref[rows, cols].astype(BF16), preferred_element_type=F32)
            gate = jnp.where(lane < HEAD_DIM, both[:CHUNK], both[CHUNK:]) + b_ref[:, cols]
            o_ref[rows, cols] = (u_ref[rows, cols] * gate).astype(BF16)


def _gmlp(z, ws_pairs, bias_tile, chunks=4):
    t = z.shape[0]
    tm = chunks * CHUNK
    return pl.pallas_call(
        functools.partial(_gmlp_kernel, chunks=chunks),
        grid=(t // tm,),
        in_specs=[pl.BlockSpec((tm, GMLP_WIDTH), lambda i: (i, 0)),
                  pl.BlockSpec((tm, GMLP_WIDTH), lambda i: (i, 1)),
                  pl.BlockSpec((GMLP_HEADS // 2, 2 * CHUNK, CHUNK), lambda i: (0, 0, 0)),
                  pl.BlockSpec((CHUNK, GMLP_WIDTH), lambda i: (0, 0))],
        out_specs=pl.BlockSpec((tm, GMLP_WIDTH), lambda i: (i, 0)),
        out_shape=jax.ShapeDtypeStruct((t, GMLP_WIDTH), BF16),
        compiler_params=_cparams(1, 32),
        name="gmlp",
    )(z, z, ws_pairs, bias_tile)


def _t5_bucket(rel):
    half = REL_BUCKETS // 2
    max_exact = half // 2
    n = np.abs(rel)
    nl = np.maximum(n, max_exact).astype(np.float32)
    large = max_exact + (np.log(nl / max_exact) / np.log(REL_MAX_DISTANCE / max_exact)
                         * (half - max_exact)).astype(np.int32)
    large = np.minimum(large, half - 1)
    b = np.where(n < max_exact, n, large) + (rel > 0).astype(np.int32) * half
    return b.astype(np.int32)


def _bias_tiles(rel_bias):
    rr = np.arange(ATTN_QB)[:, None]
    cc = np.arange(ATTN_KB)[None, :]
    out = []
    for dil in DILATIONS:
        offs = dil * np.arange(-BAND, BAND + 1, dtype=np.int32)
        band = rel_bias[jnp.asarray(_t5_bucket(offs))].T.astype(F32)
        tiles = []
        for shift in (0, -BAND, -2 * BAND):
            delta = shift + cc - rr
            valid = np.abs(delta) <= BAND
            j = np.clip(delta + BAND, 0, 2 * BAND)
            tiles.append(jnp.where(jnp.asarray(valid)[None], band[:, j], NEG_INF))
        out.append(jnp.stack(tiles, axis=0))
    return jnp.stack(out, axis=0)


def _attn_kernel(q_ref, k_ref, v_ref, bias_ref, o_ref, lse_ref, *, seq):
    head0 = lax.broadcasted_iota(jnp.int32, (ATTN_QB, LANES), 1) < HEAD_DIM
    zero = jnp.zeros((ATTN_QB, LANES), BF16)

    def rows(ref, first, n, dil):
        if dil == 1:
            return ref[0, pl.ds(pl.multiple_of(first, BAND), n), :]
        return ref[0, pl.ds(first, n, stride=dil), :]

    for pi, dil in enumerate(DILATIONS):
        sub_len = seq // dil
        nblk = sub_len // ATTN_QB

        def step(t, carry, pi=pi, dil=dil, sub_len=sub_len, nblk=nblk):
            r = t // nblk
            i = t % nblk
            kstart = jnp.clip(i * ATTN_QB - BAND, 0, sub_len - ATTN_KB)
            variant = jnp.where(i == 0, 0, jnp.where(i == nblk - 1, 2, 1))
            qfirst = r + dil * (i * ATTN_QB)
            kfirst = r + dil * kstart
            q = (rows(q_ref, qfirst, ATTN_QB, dil) * HEAD_DIM ** -0.5).astype(BF16)
            k = rows(k_ref, kfirst, ATTN_KB, dil).astype(BF16)
            v = rows(v_ref, kfirst, ATTN_KB, dil).astype(BF16)
            pv, den, lse = [], [], []
            for h in range(2):
                qh = jnp.where(head0, q, zero) if h == 0 else jnp.where(head0, zero, q)
                s = lax.dot_general(qh, k, (((1,), (1,)), ((), ())), preferred_element_type=F32)
                s = s + bias_ref[pi, variant, h]
                m = jnp.max(s, axis=-1, keepdims=True)
                p = jnp.exp(s - m)
                l = jnp.sum(p, axis=-1, keepdims=True)
                pv.append(jnp.dot(p.astype(BF16), v, preferred_element_type=F32))
                den.append(l)
                lse.append(m + jnp.log(l))
            o = jnp.where(head0, pv[0] / den[0], pv[1] / den[1])
            ls = jnp.where(head0, lse[0], lse[1])
            if dil == 1:
                idx = (0, pl.ds(pl.multiple_of(qfirst, ATTN_QB), ATTN_QB), slice(None))
                lidx = idx[1:]
            else:
                idx = (0, pl.ds(qfirst, ATTN_QB, stride=dil), slice(None))
                lidx = idx[1:]
            if pi > 0:
                ls_prev = lse_ref[lidx]
                top = jnp.maximum(ls, ls_prev)
                e_cur = jnp.exp(ls - top)
                e_prev = jnp.exp(ls_prev - top)
                tot = e_cur + e_prev
                o = (e_cur * o + e_prev * o_ref[idx]) / tot
                ls = top + jnp.log(tot)
            o_ref[idx] = o
            if pi < len(DILATIONS) - 1:
                lse_ref[lidx] = ls
            return carry

        lax.fori_loop(0, seq // ATTN_QB, step, 0)


def _attn(z3, bias):
    bsz, seq, _ = z3.shape
    assert all(seq // d >= 2 * ATTN_QB and seq % (d * ATTN_QB) == 0 for d in DILATIONS)
    col = lambda off: (lambda g, b: (b, 0, off // LANES + g))
    return pl.pallas_call(
        functools.partial(_attn_kernel, seq=seq),
        grid=(ATTN_WIDTH // LANES, bsz),
        in_specs=[pl.BlockSpec((1, seq, LANES), col(Q_OFF)),
                  pl.BlockSpec((1, seq, LANES), col(K_OFF)),
                  pl.BlockSpec((1, seq, LANES), col(V_OFF)),
                  pl.BlockSpec((len(DILATIONS), 3, LANES // HEAD_DIM, ATTN_QB, ATTN_KB),
                               lambda g, b: (0, 0, g, 0, 0))],
        out_specs=pl.BlockSpec((1, seq, LANES), lambda g, b: (b, 0, g)),
        out_shape=jax.ShapeDtypeStruct((bsz, seq, ATTN_WIDTH), F32),
        scratch_shapes=[pltpu.VMEM((seq, LANES), F32)],
        compiler_params=_cparams(2, 40),
        name="attn",
    )(z3, z3, z3, bias)


def _dft_mats():
    n = np.arange(FFT_N)
    ang = 2.0 * np.pi * np.outer(n, n) / FFT_N
    c, s = np.cos(ang), np.sin(ang)
    fa = np.concatenate([c, -s], axis=0)
    fb = np.concatenate([s, c], axis=0)
    ang_t = 2.0 * np.pi * np.outer(n, n) / (FFT_N * FFT_N)
    tw_c = np.broadcast_to(np.cos(ang_t)[:, :, None], (FFT_N, FFT_N, LANES))
    tw_s = np.broadcast_to(np.sin(ang_t)[:, :, None], (FFT_N, FFT_N, LANES))
    eye = np.eye(LANES // HEAD_DIM)
    scale = (FFT_N * FFT_N * HEAD_DIM) ** -0.5
    cc = np.kron(eye, c) * scale
    sc = np.kron(eye, s) * scale
    as_bf = lambda a: jnp.asarray(a, dtype=F32).astype(BF16)
    return (as_bf(fa), as_bf(fb), jnp.asarray(tw_c, F32), jnp.asarray(tw_s, F32), as_bf(cc), as_bf(sc))


def _fnet_kernel(x_ref, fa_ref, fb_ref, twc_ref, tws_ref, cc_ref, sc_ref, w_ref, o_ref,
                 bre_ref, bim_ref, zre_ref, zim_ref, *, epi_rows):
    def stage_a(s2, carry):
        x = x_ref[0, pl.ds(s2, FFT_N, stride=FFT_N), :].astype(BF16)
        a = jnp.dot(fa_ref[...], x, preferred_element_type=F32)
        ar, ai = a[:FFT_N], a[FFT_N:]
        c, s = twc_ref[s2], tws_ref[s2]
        dst = pl.ds(pl.multiple_of(s2 * FFT_N, FFT_N), FFT_N)
        bre_ref[dst, :] = ar * c + ai * s
        bim_ref[dst, :] = ai * c - ar * s
        return carry

    lax.fori_loop(0, FFT_N, stage_a, 0)

    def stage_b(k1, carry):
        src = pl.ds(k1, FFT_N, stride=FFT_N)
        z = (jnp.dot(fa_ref[...], bre_ref[src, :].astype(BF16), preferred_element_type=F32)
             + jnp.dot(fb_ref[...], bim_ref[src, :].astype(BF16), preferred_element_type=F32))
        zre_ref[src, :] = z[:FFT_N]
        zim_ref[src, :] = z[FFT_N:]
        return carry

    lax.fori_loop(0, FFT_N, stage_b, 0)

    for e in range(x_ref.shape[1] // epi_rows):
        rows = slice(e * epi_rows, (e + 1) * epi_rows)
        f = (jnp.dot(zre_ref[rows, :].astype(BF16), cc_ref[...], preferred_element_type=F32)
             + jnp.dot(zim_ref[rows, :].astype(BF16), sc_ref[...], preferred_element_type=F32))
        o_ref[0, rows, :] = jnp.dot(f.astype(BF16), w_ref[0], preferred_element_type=F32)


def _fnet(z3, mats, w_pairs):
    bsz, seq, _ = z3.shape
    assert seq == FFT_N * FFT_N
    fa, fb, tw_c, tw_s, cc, sc = mats
    const = lambda shape: pl.BlockSpec(shape, lambda b, j: (0,) * len(shape))
    return pl.pallas_call(
        functools.partial(_fnet_kernel, epi_rows=512),
        grid=(bsz, FNET_WIDTH // LANES),
        in_specs=[pl.BlockSpec((1, seq, LANES), lambda b, j: (b, 0, C_OFF // LANES + j)),
                  const((2 * FFT_N, FFT_N)), const((2 * FFT_N, FFT_N)),
                  const((FFT_N, FFT_N, LANES)), const((FFT_N, FFT_N, LANES)),
                  const((LANES, LANES)), const((LANES, LANES)),
                  pl.BlockSpec((1, LANES, LANES), lambda b, j: (j, 0, 0))],
        out_specs=pl.BlockSpec((1, seq, LANES), lambda b, j: (b, 0, j)),
        out_shape=jax.ShapeDtypeStruct((bsz, seq, FNET_WIDTH), F32),
        scratch_shapes=[pltpu.VMEM((seq, LANES), F32)] * 4,
        compiler_params=_cparams(2, 40),
        name="fnet",
    )(z3, fa, fb, tw_c, tw_s, cc, sc, w_pairs)


def _out_proj_kernel(a_ref, b_ref, c_ref, g_ref, w_ref, x_ref, o_ref, mix_ref):
    j = pl.program_id(1)

    @pl.when(j == 0)
    def _():
        o1, o2 = GMLP_WIDTH, GMLP_WIDTH + ATTN_WIDTH
        mix_ref[:, :o1] = _rms(a_ref[...].astype(F32), g_ref[:, :o1]).astype(BF16)
        mix_ref[:, o1:o2] = _rms(b_ref[...], g_ref[:, o1:o2]).astype(BF16)
        mix_ref[:, o2:] = _rms(c_ref[...], g_ref[:, o2:]).astype(BF16)

    o_ref[...] = x_ref[...] + jnp.dot(mix_ref[...], w_ref[...], preferred_element_type=F32)


def _out_proj(a_out, b_out, c_out, gain, w, x2, tm=512, tn=512):
    t = x2.shape[0]
    return pl.pallas_call(
        _out_proj_kernel,
        grid=(t // tm, D_MODEL // tn),
        in_specs=[pl.BlockSpec((tm, GMLP_WIDTH), lambda i, j: (i, 0)),
                  pl.BlockSpec((tm, ATTN_WIDTH), lambda i, j: (i, 0)),
                  pl.BlockSpec((tm, FNET_WIDTH), lambda i, j: (i, 0)),
                  pl.BlockSpec((1, D_MODEL), lambda i, j: (0, 0)),
                  pl.BlockSpec((D_MODEL, tn), lambda i, j: (0, j)),
                  pl.BlockSpec((tm, tn), lambda i, j: (i, j))],
        out_specs=pl.BlockSpec((tm, tn), lambda i, j: (i, j)),
        out_shape=jax.ShapeDtypeStruct((t, D_MODEL), F32),
        scratch_shapes=[pltpu.VMEM((tm, D_MODEL), BF16)],
        compiler_params=_cparams(2, 40),
        name="out_proj",
    )(a_out, b_out, c_out, gain, w, x2)


def _ffn_up_kernel(x_ref, xp_ref, xn_ref, g_ref, wg_ref, wu_ref, cwg_ref, cwu_ref, cbg_ref, cbu_ref,
                   o_ref, hn_ref, halo_ref, *, tiles_per_seq):
    i = pl.program_id(0)
    j = pl.program_id(1)
    tm = x_ref.shape[0]

    @pl.when(j == 0)
    def _():
        hn_ref[...] = _rms(x_ref[...], g_ref[...]).astype(BF16)
        halo_ref[:HALO] = _rms(xp_ref[...], g_ref[...]).astype(BF16)
        halo_ref[HALO:] = _rms(xn_ref[...], g_ref[...]).astype(BF16)

    has_prev = (i % tiles_per_seq != 0).astype(F32)
    has_next = (i % tiles_per_seq != tiles_per_seq - 1).astype(F32)
    row = lax.broadcasted_iota(jnp.int32, (tm, 1), 0)

    def branch(w_ref, cw_ref, cb_ref):
        h = jnp.dot(hn_ref[...], w_ref[...], preferred_element_type=F32)
        hh = jnp.dot(halo_ref[...], w_ref[...], preferred_element_type=F32)
        prev_row = hh[HALO - 1:HALO] * has_prev
        next_row = hh[HALO:HALO + 1] * has_next
        up = jnp.where(row == 0, prev_row, pltpu.roll(h, 1, axis=0))
        dn = jnp.where(row == tm - 1, next_row, pltpu.roll(h, tm - 1, axis=0))
        return up * cw_ref[0:1] + h * cw_ref[1:2] + dn * cw_ref[2:3] + cb_ref[...]

    gate = branch(wg_ref, cwg_ref, cbg_ref)
    up = branch(wu_ref, cwu_ref, cbu_ref)
    o_ref[...] = (gate * jax.nn.sigmoid(gate) * up).astype(BF16)


def _ffn_up(x2, gain, w, conv_w, conv_b, seq, tm=512, tn=512):
    t = x2.shape[0]
    nj = D_FF // tn
    hb = tm // HALO
    last = t // HALO - 1
    return pl.pallas_call(
        functools.partial(_ffn_up_kernel, tiles_per_seq=seq // tm),
        grid=(t // tm, nj),
        in_specs=[pl.BlockSpec((tm, D_MODEL), lambda i, j: (i, 0)),
                  pl.BlockSpec((HALO, D_MODEL), lambda i, j: (jnp.maximum(i * hb - 1, 0), 0)),
                  pl.BlockSpec((HALO, D_MODEL), lambda i, j: (jnp.minimum((i + 1) * hb, last), 0)),
                  pl.BlockSpec((1, D_MODEL), lambda i, j: (0, 0)),
                  pl.BlockSpec((D_MODEL, tn), lambda i, j: (0, j)),
                  pl.BlockSpec((D_MODEL, tn), lambda i, j: (0, j + nj)),
                  pl.BlockSpec((3, tn), lambda i, j: (0, j)),
                  pl.BlockSpec((3, tn), lambda i, j: (0, j + nj)),
                  pl.BlockSpec((1, tn), lambda i, j: (0, j)),
                  pl.BlockSpec((1, tn), lambda i, j: (0, j + nj))],
        out_specs=pl.BlockSpec((tm, tn), lambda i, j: (i, j)),
        out_shape=jax.ShapeDtypeStruct((t, D_FF), BF16),
        scratch_shapes=[pltpu.VMEM((tm, D_MODEL), BF16), pltpu.VMEM((2 * HALO, D_MODEL), BF16)],
        compiler_params=_cparams(2, 40),
        name="ffn_up",
    )(x2, x2, x2, gain, w, w, conv_w, conv_w, conv_b, conv_b)


def _ffn_down_kernel(a_ref, w_ref, x_ref, o_ref):
    o_ref[...] = x_ref[...] + jnp.dot(a_ref[...], w_ref[...], preferred_element_type=F32)


def _ffn_down(act, w, x2, tm=512, tn=512):
    t = x2.shape[0]
    return pl.pallas_call(
        _ffn_down_kernel,
        grid=(t // tm, D_MODEL // tn),
        in_specs=[pl.BlockSpec((tm, D_FF), lambda i, j: (i, 0)),
                  pl.BlockSpec((D_FF, tn), lambda i, j: (0, j)),
                  pl.BlockSpec((tm, tn), lambda i, j: (i, j))],
        out_specs=pl.BlockSpec((tm, tn), lambda i, j: (i, j)),
        out_shape=jax.ShapeDtypeStruct((t, D_MODEL), F32),
        compiler_params=_cparams(2, 48),
        name="ffn_down",
    )(act, w, x2)


def _final_norm_kernel(x_ref, g_ref, o_ref):
    o_ref[...] = _rms(x_ref[...], g_ref[...])


def _final_norm(x2, gain, tm=512):
    t = x2.shape[0]
    return pl.pallas_call(
        _final_norm_kernel,
        grid=(t // tm,),
        in_specs=[pl.BlockSpec((tm, D_MODEL), lambda i: (i, 0)),
                  pl.BlockSpec((1, D_MODEL), lambda i: (0, 0))],
        out_specs=pl.BlockSpec((tm, D_MODEL), lambda i: (i, 0)),
        out_shape=jax.ShapeDtypeStruct((t, D_MODEL), F32),
        compiler_params=_cparams(1, 32),
        name="final_norm",
    )(x2, gain)


def kernel(x, w_in, gmlp_ws, gmlp_b, fnet_w, mix_gain, w_out, norm_mix, norm_ffn, ffn_up, ffn_conv_w,
           ffn_conv_b, ffn_down, rel_bias, final_norm):
    bsz, seq, _ = x.shape
    depth = w_in.shape[0]
    t = bsz * seq
    x2 = x.reshape(t, D_MODEL)
    mats = _dft_mats()
    bias = _bias_tiles(rel_bias)
    eye = jnp.eye(LANES // HEAD_DIM, dtype=F32)
    for l in range(depth):
        z = _proj_in(x2, norm_mix[l][None], w_in[l].astype(BF16))
        ws_pairs = gmlp_ws[l].astype(BF16).reshape(GMLP_HEADS // 2, 2 * CHUNK, CHUNK)
        b_tile = jnp.repeat(gmlp_b[l].T, HEAD_DIM, axis=1)
        a_out = _gmlp(z, ws_pairs, b_tile)
        z3 = z.reshape(bsz, seq, IN_WIDTH)
        b_out = _attn(z3, bias).reshape(t, ATTN_WIDTH)
        w_pairs = jnp.einsum("gh,pgce->pgche", eye, fnet_w[l].reshape(-1, 2, HEAD_DIM, HEAD_DIM))
        w_pairs = w_pairs.reshape(-1, LANES, LANES).astype(BF16)
        c_out = _fnet(z3, mats, w_pairs).reshape(t, FNET_WIDTH)
        x2 = _out_proj(a_out, b_out, c_out, mix_gain[l][None], w_out[l].astype(BF16), x2)
        act = _ffn_up(x2, norm_ffn[l][None], ffn_up[l].astype(BF16), ffn_conv_w[l], ffn_conv_b[l][None], seq)
        x2 = _ffn_down(act, ffn_down[l].astype(BF16), x2)
    return _final_norm(x2, final_norm[None]).reshape(bsz, seq, D_MODEL)
```

```python
import functools

import numpy as np
import jax
import jax.numpy as jnp
from jax import lax
from jax.experimental import pallas as pl
from jax.experimental.pallas import tpu as pltpu

D_MODEL = 2048
HEAD_DIM = 64
GMLP_WIDTH = 512
ATTN_WIDTH = 1024
FNET_WIDTH = 512
GMLP_HEADS = GMLP_WIDTH // HEAD_DIM
ATTN_HEADS = ATTN_WIDTH // HEAD_DIM
FNET_GROUPS = FNET_WIDTH // HEAD_DIM
CHUNK = 128
DILATIONS = (1, 4, 16)
REL_BUCKETS = 32
REL_MAX_DISTANCE = 1024
D_FF = 5632
EPS = 1e-6
IN_WIDTH = 2 * GMLP_WIDTH + 3 * ATTN_WIDTH + FNET_WIDTH
NEG_INF = -1e30
LOG2E = 1.4426950408889634

Q_OFF = 2 * GMLP_WIDTH
K_OFF = Q_OFF + ATTN_WIDTH
V_OFF = K_OFF + ATTN_WIDTH
C_OFF = V_OFF + ATTN_WIDTH

LANES = 128
BAND = 64
ATTN_QB = 128
ATTN_KB = ATTN_QB + 2 * BAND
FFT_N = 64
HALO = 8
BIAS_PAD = 512
BIAS_SHIFTS = (BIAS_PAD - BAND, 0, BAND)

F32 = jnp.float32
BF16 = jnp.bfloat16
MIB = 1024 * 1024


def _cparams(n_axes, vmem_mib):
    return pltpu.CompilerParams(dimension_semantics=("arbitrary",) * n_axes,
                                vmem_limit_bytes=vmem_mib * MIB)


def _rms(x, g):
    return x * lax.rsqrt(jnp.mean(x * x, axis=-1, keepdims=True) + EPS) * g


def _proj_in_kernel(x_ref, g_ref, w_ref, o_ref, xn_ref, *, gelu_tiles):
    j = pl.program_id(1)

    @pl.when(j == 0)
    def _():
        xn_ref[...] = _rms(x_ref[...], g_ref[...]).astype(BF16)

    acc = jnp.dot(xn_ref[...], w_ref[...], preferred_element_type=F32)

    @pl.when(j < gelu_tiles)
    def _():
        o_ref[...] = 0.5 * acc * (1.0 + lax.erf(acc * np.float32(2.0 ** -0.5)))

    @pl.when(j >= gelu_tiles)
    def _():
        o_ref[...] = acc


def _proj_in(x2, gain, w, tm=512, tn=512):
    t = x2.shape[0]
    return pl.pallas_call(
        functools.partial(_proj_in_kernel, gelu_tiles=Q_OFF // tn),
        grid=(t // tm, IN_WIDTH // tn),
        in_specs=[pl.BlockSpec((tm, D_MODEL), lambda i, j: (i, 0)),
                  pl.BlockSpec((1, D_MODEL), lambda i, j: (0, 0)),
                  pl.BlockSpec((D_MODEL, tn), lambda i, j: (0, j))],
        out_specs=pl.BlockSpec((tm, tn), lambda i, j: (i, j)),
        out_shape=jax.ShapeDtypeStruct((t, IN_WIDTH), F32),
        scratch_shapes=[pltpu.VMEM((tm, D_MODEL), BF16)],
        compiler_params=_cparams(2, 40),
        name="proj_in",
    )(x2, gain, w)


def _gmlp_kernel(u_ref, v_ref, ws_ref, b_ref, o_ref, *, chunks):
    lane = lax.broadcasted_iota(jnp.int32, (CHUNK, 2 * HEAD_DIM), 1)
    for c in range(chunks):
        rows = slice(c * CHUNK, (c + 1) * CHUNK)
        for p in range(GMLP_HEADS // 2):
            cols = slice(p * 2 * HEAD_DIM, (p + 1) * 2 * HEAD_DIM)
            both = jnp.dot(ws_ref[p], v_ref[rows, cols].astype(BF16), preferred_element_type=F32)
            gate = jnp.where(lane < HEAD_DIM, both[:CHUNK], both[CHUNK:]) + b_ref[:, cols]
            o_ref[rows, cols] = (u_ref[rows, cols] * gate).astype(BF16)


def _gmlp(z, ws_pairs, bias_tile, chunks=4):
    t = z.shape[0]
    tm = chunks * CHUNK
    return pl.pallas_call(
        functools.partial(_gmlp_kernel, chunks=chunks),
        grid=(t // tm,),
        in_specs=[pl.BlockSpec((tm, GMLP_WIDTH), lambda i: (i, 0)),
                  pl.BlockSpec((tm, GMLP_WIDTH), lambda i: (i, 1)),
                  pl.BlockSpec((GMLP_HEADS // 2, 2 * CHUNK, CHUNK), lambda i: (0, 0, 0)),
                  pl.BlockSpec((CHUNK, GMLP_WIDTH), lambda i: (0, 0))],
        out_specs=pl.BlockSpec((tm, GMLP_WIDTH), lambda i: (i, 0)),
        out_shape=jax.ShapeDtypeStruct((t, GMLP_WIDTH), BF16),
        compiler_params=_cparams(1, 32),
        name="gmlp",
    )(z, z, ws_pairs, bias_tile)


def _t5_bucket(rel):
    half = REL_BUCKETS // 2
    max_exact = half // 2
    n = np.abs(rel)
    nl = np.maximum(n, max_exact).astype(np.float32)
    large = max_exact + (np.log(nl / max_exact) / np.log(REL_MAX_DISTANCE / max_exact)
                         * (half - max_exact)).astype(np.int32)
    large = np.minimum(large, half - 1)
    b = np.where(n < max_exact, n, large) + (rel > 0).astype(np.int32) * half
    return b.astype(np.int32)


def _bias_kernel(band_ref, o_ref):
    x = jnp.broadcast_to(band_ref[0, 0] * LOG2E, (ATTN_QB, BIAS_PAD))
    for variant, shift in enumerate(BIAS_SHIFTS):
        t = pltpu.roll(x, shift, 1, stride=1, stride_axis=0)
        o_ref[0, variant, 0] = t[:, :ATTN_KB]


def _bias_tiles(rel_bias):
    bands = []
    for dil in DILATIONS:
        offs = dil * np.arange(-BAND, BAND + 1, dtype=np.int32)
        bands.append(rel_bias[jnp.asarray(_t5_bucket(offs))].T.astype(F32))
    band = jnp.pad(jnp.stack(bands), ((0, 0), (0, 0), (0, BIAS_PAD - 2 * BAND - 1)), constant_values=NEG_INF)
    return pl.pallas_call(
        _bias_kernel,
        grid=(len(DILATIONS), ATTN_HEADS),
        in_specs=[pl.BlockSpec((1, 1, 1, BIAS_PAD), lambda p, h: (p, h, 0, 0))],
        out_specs=pl.BlockSpec((1, 3, 1, ATTN_QB, ATTN_KB), lambda p, h: (p, 0, h, 0, 0)),
        out_shape=jax.ShapeDtypeStruct((len(DILATIONS), 3, ATTN_HEADS, ATTN_QB, ATTN_KB), F32),
        compiler_params=_cparams(2, 32),
        name="bias_tiles",
    )(band[:, :, None, :])


def _attn_kernel(q_ref, k_ref, v_ref, bias_ref, o_ref, m_ref, l_ref, qd_ref, kd_ref, vd_ref, *, seq, unroll):
    head0 = lax.broadcasted_iota(jnp.int32, (ATTN_QB, LANES), 1) < HEAD_DIM
    zero = jnp.zeros((ATTN_QB, LANES), BF16)
    nsteps = seq // ATTN_QB

    for pi, dil in enumerate(DILATIONS):
        sub_len = seq // dil
        nblk = sub_len // ATTN_QB

        def natural_rows(t, dil=dil, nblk=nblk):
            if dil == 1:
                return pl.ds(pl.multiple_of(t * ATTN_QB, ATTN_QB), ATTN_QB)
            return pl.ds(t // nblk + dil * ATTN_QB * (t % nblk), ATTN_QB, stride=dil)

        def gather(t, carry, natural_rows=natural_rows):
            dst = pl.ds(pl.multiple_of(t * ATTN_QB, ATTN_QB), ATTN_QB)
            src = natural_rows(t)
            qd_ref[dst, :] = (q_ref[0, src, :] * (HEAD_DIM ** -0.5 * LOG2E)).astype(BF16)
            kd_ref[dst, :] = k_ref[0, src, :].astype(BF16)
            vd_ref[dst, :] = v_ref[0, src, :].astype(BF16)
            return carry

        lax.fori_loop(0, nsteps, gather, 0, unroll=unroll)

        def step(t, carry, pi=pi, sub_len=sub_len, nblk=nblk, natural_rows=natural_rows):
            i = t % nblk
            kstart = jnp.clip(i * ATTN_QB - BAND, 0, sub_len - ATTN_KB)
            variant = jnp.where(i == 0, 0, jnp.where(i == nblk - 1, 2, 1))
            krows = pl.ds(pl.multiple_of((t - i) * ATTN_QB + kstart, BAND), ATTN_KB)
            q = qd_ref[pl.ds(pl.multiple_of(t * ATTN_QB, ATTN_QB), ATTN_QB), :]
            k = kd_ref[krows, :]
            v = vd_ref[krows, :]
            pv, den, top = [], [], []
            for h in range(2):
                qh = jnp.where(head0, q, zero) if h == 0 else jnp.where(head0, zero, q)
                s = lax.dot_general(qh, k, (((1,), (1,)), ((), ())), preferred_element_type=F32)
                s = s + bias_ref[pi, variant, h]
                m = jnp.max(s, axis=-1, keepdims=True)
                p = jnp.exp2(s - m)
                pv.append(jnp.dot(p.astype(BF16), v, preferred_element_type=F32))
                den.append(jnp.sum(p, axis=-1, keepdims=True))
                top.append(m)
            acc = jnp.where(head0, pv[0], pv[1])
            l = jnp.where(head0, den[0], den[1])
            m = jnp.where(head0, top[0], top[1])
            dst = natural_rows(t)
            if pi > 0:
                m_run = m_ref[dst, :]
                m_new = jnp.maximum(m, m_run)
                w_cur = jnp.exp2(m - m_new)
                w_run = jnp.exp2(m_run - m_new)
                acc = w_cur * acc + w_run * o_ref[0, dst, :]
                l = w_cur * l + w_run * l_ref[dst, :]
                m = m_new
            if pi < len(DILATIONS) - 1:
                o_ref[0, dst, :] = acc
                m_ref[dst, :] = m
                l_ref[dst, :] = l
            else:
                o_ref[0, dst, :] = acc / l
            return carry

        lax.fori_loop(0, nsteps, step, 0, unroll=unroll)


def _attn(z3, bias, unroll=8):
    bsz, seq, _ = z3.shape
    assert all(seq // d >= 2 * ATTN_QB and seq % (d * ATTN_QB) == 0 for d in DILATIONS)
    col = lambda off: (lambda g, b: (b, 0, off // LANES + g))
    return pl.pallas_call(
        functools.partial(_attn_kernel, seq=seq, unroll=unroll),
        grid=(ATTN_WIDTH // LANES, bsz),
        in_specs=[pl.BlockSpec((1, seq, LANES), col(Q_OFF)),
                  pl.BlockSpec((1, seq, LANES), col(K_OFF)),
                  pl.BlockSpec((1, seq, LANES), col(V_OFF)),
                  pl.BlockSpec((len(DILATIONS), 3, LANES // HEAD_DIM, ATTN_QB, ATTN_KB),
                               lambda g, b: (0, 0, g, 0, 0))],
        out_specs=pl.BlockSpec((1, seq, LANES), lambda g, b: (b, 0, g)),
        out_shape=jax.ShapeDtypeStruct((bsz, seq, ATTN_WIDTH), F32),
        scratch_shapes=[pltpu.VMEM((seq, LANES), F32)] * 2 + [pltpu.VMEM((seq, LANES), BF16)] * 3,
        compiler_params=_cparams(2, 40),
        name="attn",
    )(z3, z3, z3, bias)


def _dft_mats():
    n = np.arange(FFT_N)
    ang = 2.0 * np.pi * np.outer(n, n) / FFT_N
    c, s = np.cos(ang), np.sin(ang)
    fa = np.concatenate([c, -s], axis=0)
    fb = np.concatenate([s, c], axis=0)
    ang_t = 2.0 * np.pi * np.outer(n, n) / (FFT_N * FFT_N)
    tw_c = np.broadcast_to(np.cos(ang_t)[:, :, None], (FFT_N, FFT_N, LANES))
    tw_s = np.broadcast_to(np.sin(ang_t)[:, :, None], (FFT_N, FFT_N, LANES))
    eye = np.eye(LANES // HEAD_DIM)
    scale = (FFT_N * FFT_N * HEAD_DIM) ** -0.5
    cc = np.kron(eye, c) * scale
    sc = np.kron(eye, s) * scale
    as_bf = lambda a: jnp.asarray(a, dtype=F32).astype(BF16)
    return (as_bf(fa), as_bf(fb), jnp.asarray(tw_c, F32), jnp.asarray(tw_s, F32), as_bf(cc), as_bf(sc))


def _fnet_kernel(x_ref, fa_ref, fb_ref, twc_ref, tws_ref, cc_ref, sc_ref, w_ref, o_ref,
                 bre_ref, bim_ref, zre_ref, zim_ref, *, epi_rows):
    def stage_a(s2, carry):
        x = x_ref[0, pl.ds(s2, FFT_N, stride=FFT_N), :].astype(BF16)
        a = jnp.dot(fa_ref[...], x, preferred_element_type=F32)
        ar, ai = a[:FFT_N], a[FFT_N:]
        c, s = twc_ref[s2], tws_ref[s2]
        dst = pl.ds(pl.multiple_of(s2 * FFT_N, FFT_N), FFT_N)
        bre_ref[dst, :] = ar * c + ai * s
        bim_ref[dst, :] = ai * c - ar * s
        return carry

    lax.fori_loop(0, FFT_N, stage_a, 0)

    def stage_b(k1, carry):
        src = pl.ds(k1, FFT_N, stride=FFT_N)
        z = (jnp.dot(fa_ref[...], bre_ref[src, :].astype(BF16), preferred_element_type=F32)
             + jnp.dot(fb_ref[...], bim_ref[src, :].astype(BF16), preferred_element_type=F32))
        zre_ref[src, :] = z[:FFT_N]
        zim_ref[src, :] = z[FFT_N:]
        return carry

    lax.fori_loop(0, FFT_N, stage_b, 0)

    for e in range(x_ref.shape[1] // epi_rows):
        rows = slice(e * epi_rows, (e + 1) * epi_rows)
        f = (jnp.dot(zre_ref[rows, :].astype(BF16), cc_ref[...], preferred_element_type=F32)
             + jnp.dot(zim_ref[rows, :].astype(BF16), sc_ref[...], preferred_element_type=F32))
        o_ref[0, rows, :] = jnp.dot(f.astype(BF16), w_ref[0], preferred_element_type=F32)


def _fnet(z3, mats, w_pairs):
    bsz, seq, _ = z3.shape
    assert seq == FFT_N * FFT_N
    fa, fb, tw_c, tw_s, cc, sc = mats
    const = lambda shape: pl.BlockSpec(shape, lambda b, j: (0,) * len(shape))
    return pl.pallas_call(
        functools.partial(_fnet_kernel, epi_rows=512),
        grid=(bsz, FNET_WIDTH // LANES),
        in_specs=[pl.BlockSpec((1, seq, LANES), lambda b, j: (b, 0, C_OFF // LANES + j)),
                  const((2 * FFT_N, FFT_N)), const((2 * FFT_N, FFT_N)),
                  const((FFT_N, FFT_N, LANES)), const((FFT_N, FFT_N, LANES)),
                  const((LANES, LANES)), const((LANES, LANES)),
                  pl.BlockSpec((1, LANES, LANES), lambda b, j: (j, 0, 0))],
        out_specs=pl.BlockSpec((1, seq, LANES), lambda b, j: (b, 0, j)),
        out_shape=jax.ShapeDtypeStruct((bsz, seq, FNET_WIDTH), F32),
        scratch_shapes=[pltpu.VMEM((seq, LANES), F32)] * 4,
        compiler_params=_cparams(2, 40),
        name="fnet",
    )(z3, fa, fb, tw_c, tw_s, cc, sc, w_pairs)


def _out_proj_kernel(a_ref, b_ref, c_ref, g_ref, w_ref, x_ref, o_ref, mix_ref):
    j = pl.program_id(1)

    @pl.when(j == 0)
    def _():
        o1, o2 = GMLP_WIDTH, GMLP_WIDTH + ATTN_WIDTH
        mix_ref[:, :o1] = _rms(a_ref[...].astype(F32), g_ref[:, :o1]).astype(BF16)
        mix_ref[:, o1:o2] = _rms(b_ref[...], g_ref[:, o1:o2]).astype(BF16)
        mix_ref[:, o2:] = _rms(c_ref[...], g_ref[:, o2:]).astype(BF16)

    o_ref[...] = x_ref[...] + jnp.dot(mix_ref[...], w_ref[...], preferred_element_type=F32)


def _out_proj(a_out, b_out, c_out, gain, w, x2, tm=512, tn=512):
    t = x2.shape[0]
    return pl.pallas_call(
        _out_proj_kernel,
        grid=(t // tm, D_MODEL // tn),
        in_specs=[pl.BlockSpec((tm, GMLP_WIDTH), lambda i, j: (i, 0)),
                  pl.BlockSpec((tm, ATTN_WIDTH), lambda i, j: (i, 0)),
                  pl.BlockSpec((tm, FNET_WIDTH), lambda i, j: (i, 0)),
                  pl.BlockSpec((1, D_MODEL), lambda i, j: (0, 0)),
                  pl.BlockSpec((D_MODEL, tn), lambda i, j: (0, j)),
                  pl.BlockSpec((tm, tn), lambda i, j: (i, j))],
        out_specs=pl.BlockSpec((tm, tn), lambda i, j: (i, j)),
        out_shape=jax.ShapeDtypeStruct((t, D_MODEL), F32),
        scratch_shapes=[pltpu.VMEM((tm, D_MODEL), BF16)],
        compiler_params=_cparams(2, 40),
        name="out_proj",
    )(a_out, b_out, c_out, gain, w, x2)


def _ffn_up_kernel(x_ref, xp_ref, xn_ref, g_ref, wg_ref, wu_ref, cwg_ref, cwu_ref, cbg_ref, cbu_ref,
                   o_ref, hn_ref, halo_ref, *, tiles_per_seq):
    i = pl.program_id(0)
    j = pl.program_id(1)
    tm = x_ref.shape[0]

    @pl.when(j == 0)
    def _():
        hn_ref[...] = _rms(x_ref[...], g_ref[...]).astype(BF16)
        halo_ref[:HALO] = _rms(xp_ref[...], g_ref[...]).astype(BF16)
        halo_ref[HALO:] = _rms(xn_ref[...], g_ref[...]).astype(BF16)

    has_prev = (i % tiles_per_seq != 0).astype(F32)
    has_next = (i % tiles_per_seq != tiles_per_seq - 1).astype(F32)
    row = lax.broadcasted_iota(jnp.int32, (tm, 1), 0)

    def branch(w_ref, cw_ref, cb_ref):
        h = jnp.dot(hn_ref[...], w_ref[...], preferred_element_type=F32)
        hh = jnp.dot(halo_ref[...], w_ref[...], preferred_element_type=F32)
        prev_row = hh[HALO - 1:HALO] * has_prev
        next_row = hh[HALO:HALO + 1] * has_next
        up = jnp.where(row == 0, prev_row, pltpu.roll(h, 1, axis=0))
        dn = jnp.where(row == tm - 1, next_row, pltpu.roll(h, tm - 1, axis=0))
        return up * cw_ref[0:1] + h * cw_ref[1:2] + dn * cw_ref[2:3] + cb_ref[...]

    gate = branch(wg_ref, cwg_ref, cbg_ref)
    up = branch(wu_ref, cwu_ref, cbu_ref)
    o_ref[...] = (gate * jax.nn.sigmoid(gate) * up).astype(BF16)


def _ffn_up(x2, gain, w, conv_w, conv_b, seq, tm=512, tn=512):
    t = x2.shape[0]
    nj = D_FF // tn
    hb = tm // HALO
    last = t // HALO - 1
    return pl.pallas_call(
        functools.partial(_ffn_up_kernel, tiles_per_seq=seq // tm),
        grid=(t // tm, nj),
        in_specs=[pl.BlockSpec((tm, D_MODEL), lambda i, j: (i, 0)),
                  pl.BlockSpec((HALO, D_MODEL), lambda i, j: (jnp.maximum(i * hb - 1, 0), 0)),
                  pl.BlockSpec((HALO, D_MODEL), lambda i, j: (jnp.minimum((i + 1) * hb, last), 0)),
                  pl.BlockSpec((1, D_MODEL), lambda i, j: (0, 0)),
                  pl.BlockSpec((D_MODEL, tn), lambda i, j: (0, j)),
                  pl.BlockSpec((D_MODEL, tn), lambda i, j: (0, j + nj)),
                  pl.BlockSpec((3, tn), lambda i, j: (0, j)),
                  pl.BlockSpec((3, tn), lambda i, j: (0, j + nj)),
                  pl.BlockSpec((1, tn), lambda i, j: (0, j)),
                  pl.BlockSpec((1, tn), lambda i, j: (0, j + nj))],
        out_specs=pl.BlockSpec((tm, tn), lambda i, j: (i, j)),
        out_shape=jax.ShapeDtypeStruct((t, D_FF), BF16),
        scratch_shapes=[pltpu.VMEM((tm, D_MODEL), BF16), pltpu.VMEM((2 * HALO, D_MODEL), BF16)],
        compiler_params=_cparams(2, 40),
        name="ffn_up",
    )(x2, x2, x2, gain, w, w, conv_w, conv_w, conv_b, conv_b)


def _ffn_down_kernel(a_ref, w_ref, x_ref, o_ref):
    o_ref[...] = x_ref[...] + jnp.dot(a_ref[...], w_ref[...], preferred_element_type=F32)


def _ffn_down(act, w, x2, tm=512, tn=512):
    t = x2.shape[0]
    return pl.pallas_call(
        _ffn_down_kernel,
        grid=(t // tm, D_MODEL // tn),
        in_specs=[pl.BlockSpec((tm, D_FF), lambda i, j: (i, 0)),
                  pl.BlockSpec((D_FF, tn), lambda i, j: (0, j)),
                  pl.BlockSpec((tm, tn), lambda i, j: (i, j))],
        out_specs=pl.BlockSpec((tm, tn), lambda i, j: (i, j)),
        out_shape=jax.ShapeDtypeStruct((t, D_MODEL), F32),
        compiler_params=_cparams(2, 48),
        name="ffn_down",
    )(act, w, x2)


def _final_norm_kernel(x_ref, g_ref, o_ref):
    o_ref[...] = _rms(x_ref[...], g_ref[...])


def _final_norm(x2, gain, tm=512):
    t = x2.shape[0]
    return pl.pallas_call(
        _final_norm_kernel,
        grid=(t // tm,),
        in_specs=[pl.BlockSpec((tm, D_MODEL), lambda i: (i, 0)),
                  pl.BlockSpec((1, D_MODEL), lambda i: (0, 0))],
        out_specs=pl.BlockSpec((tm, D_MODEL), lambda i: (i, 0)),
        out_shape=jax.ShapeDtypeStruct((t, D_MODEL), F32),
        compiler_params=_cparams(1, 32),
        name="final_norm",
    )(x2, gain)


def kernel(x, w_in, gmlp_ws, gmlp_b, fnet_w, mix_gain, w_out, norm_mix, norm_ffn, ffn_up, ffn_conv_w,
           ffn_conv_b, ffn_down, rel_bias, final_norm):
    bsz, seq, _ = x.shape
    depth = w_in.shape[0]
    t = bsz * seq
    x2 = x.reshape(t, D_MODEL)
    mats = _dft_mats()
    bias = _bias_tiles(rel_bias)
    eye = jnp.eye(LANES // HEAD_DIM, dtype=F32)
    for l in range(depth):
        z = _proj_in(x2, norm_mix[l][None], w_in[l].astype(BF16))
        ws_pairs = gmlp_ws[l].astype(BF16).reshape(GMLP_HEADS // 2, 2 * CHUNK, CHUNK)
        b_tile = jnp.repeat(gmlp_b[l].T, HEAD_DIM, axis=1)
        a_out = _gmlp(z, ws_pairs, b_tile)
        z3 = z.reshape(bsz, seq, IN_WIDTH)
        b_out = _attn(z3, bias).reshape(t, ATTN_WIDTH)
        w_pairs = jnp.einsum("gh,pgce->pgche", eye, fnet_w[l].reshape(-1, 2, HEAD_DIM, HEAD_DIM))
        w_pairs = w_pairs.reshape(-1, LANES, LANES).astype(BF16)
        c_out = _fnet(z3, mats, w_pairs).reshape(t, FNET_WIDTH)
        x2 = _out_proj(a_out, b_out, c_out, mix_gain[l][None], w_out[l].astype(BF16), x2)
        act = _ffn_up(x2, norm_ffn[l][None], ffn_up[l].astype(BF16), ffn_conv_w[l], ffn_conv_b[l][None], seq)
        x2 = _ffn_down(act, ffn_down[l].astype(BF16), x2)
    return _final_norm(x2, final_norm[None]).reshape(bsz, seq, D_MODEL)
```

```python
import functools

import numpy as np
import jax
import jax.numpy as jnp
from jax import lax
from jax.experimental import pallas as pl
from jax.experimental.pallas import tpu as pltpu

D_MODEL = 2048
HEAD_DIM = 64
GMLP_WIDTH = 512
ATTN_WIDTH = 1024
FNET_WIDTH = 512
GMLP_HEADS = GMLP_WIDTH // HEAD_DIM
ATTN_HEADS = ATTN_WIDTH // HEAD_DIM
FNET_GROUPS = FNET_WIDTH // HEAD_DIM
CHUNK = 128
DILATIONS = (1, 4, 16)
REL_BUCKETS = 32
REL_MAX_DISTANCE = 1024
D_FF = 5632
EPS = 1e-6
IN_WIDTH = 2 * GMLP_WIDTH + 3 * ATTN_WIDTH + FNET_WIDTH
NEG_INF = -1e30
LOG2E = 1.4426950408889634

Q_OFF = 2 * GMLP_WIDTH
K_OFF = Q_OFF + ATTN_WIDTH
V_OFF = K_OFF + ATTN_WIDTH
C_OFF = V_OFF + ATTN_WIDTH

LANES = 128
BAND = 64
ATTN_QB = 128
ATTN_KB = ATTN_QB + 2 * BAND
FFT_N = 64
HALO = 8
BIAS_PAD = 512
BIAS_SHIFTS = (BIAS_PAD - BAND, 0, BAND)

F32 = jnp.float32
BF16 = jnp.bfloat16
MIB = 1024 * 1024


def _cparams(n_axes, vmem_mib):
    return pltpu.CompilerParams(dimension_semantics=("arbitrary",) * n_axes,
                                vmem_limit_bytes=vmem_mib * MIB)


def _rms(x, g):
    return x * lax.rsqrt(jnp.mean(x * x, axis=-1, keepdims=True) + EPS) * g


def _proj_in_kernel(x_ref, g_ref, w_ref, o_ref, xn_ref, *, gelu_tiles):
    j = pl.program_id(1)

    @pl.when(j == 0)
    def _():
        xn_ref[...] = _rms(x_ref[...], g_ref[...]).astype(BF16)

    acc = jnp.dot(xn_ref[...], w_ref[...], preferred_element_type=F32)

    @pl.when(j < gelu_tiles)
    def _():
        o_ref[...] = 0.5 * acc * (1.0 + lax.erf(acc * np.float32(2.0 ** -0.5)))

    @pl.when(j >= gelu_tiles)
    def _():
        o_ref[...] = acc


def _proj_in(x2, gain, w, tm=1024, tn=512):
    t = x2.shape[0]
    return pl.pallas_call(
        functools.partial(_proj_in_kernel, gelu_tiles=Q_OFF // tn),
        grid=(t // tm, IN_WIDTH // tn),
        in_specs=[pl.BlockSpec((tm, D_MODEL), lambda i, j: (i, 0)),
                  pl.BlockSpec((1, D_MODEL), lambda i, j: (0, 0)),
                  pl.BlockSpec((D_MODEL, tn), lambda i, j: (0, j))],
        out_specs=pl.BlockSpec((tm, tn), lambda i, j: (i, j)),
        out_shape=jax.ShapeDtypeStruct((t, IN_WIDTH), F32),
        scratch_shapes=[pltpu.VMEM((tm, D_MODEL), BF16)],
        compiler_params=_cparams(2, 56),
        name="proj_in",
    )(x2, gain, w)


def _gmlp_kernel(u_ref, v_ref, ws_ref, b_ref, o_ref, *, chunks):
    lane = lax.broadcasted_iota(jnp.int32, (CHUNK, 2 * HEAD_DIM), 1)
    for c in range(chunks):
        rows = slice(c * CHUNK, (c + 1) * CHUNK)
        for p in range(GMLP_HEADS // 2):
            cols = slice(p * 2 * HEAD_DIM, (p + 1) * 2 * HEAD_DIM)
            both = jnp.dot(ws_ref[p], v_ref[rows, cols].astype(BF16), preferred_element_type=F32)
            gate = jnp.where(lane < HEAD_DIM, both[:CHUNK], both[CHUNK:]) + b_ref[:, cols]
            o_ref[rows, cols] = (u_ref[rows, cols] * gate).astype(BF16)


def _gmlp(z, ws_pairs, bias_tile, chunks=4):
    t = z.shape[0]
    tm = chunks * CHUNK
    return pl.pallas_call(
        functools.partial(_gmlp_kernel, chunks=chunks),
        grid=(t // tm,),
        in_specs=[pl.BlockSpec((tm, GMLP_WIDTH), lambda i: (i, 0)),
                  pl.BlockSpec((tm, GMLP_WIDTH), lambda i: (i, 1)),
                  pl.BlockSpec((GMLP_HEADS // 2, 2 * CHUNK, CHUNK), lambda i: (0, 0, 0)),
                  pl.BlockSpec((CHUNK, GMLP_WIDTH), lambda i: (0, 0))],
        out_specs=pl.BlockSpec((tm, GMLP_WIDTH), lambda i: (i, 0)),
        out_shape=jax.ShapeDtypeStruct((t, GMLP_WIDTH), BF16),
        compiler_params=_cparams(1, 32),
        name="gmlp",
    )(z, z, ws_pairs, bias_tile)


def _t5_bucket(rel):
    half = REL_BUCKETS // 2
    max_exact = half // 2
    n = np.abs(rel)
    nl = np.maximum(n, max_exact).astype(np.float32)
    large = max_exact + (np.log(nl / max_exact) / np.log(REL_MAX_DISTANCE / max_exact)
                         * (half - max_exact)).astype(np.int32)
    large = np.minimum(large, half - 1)
    b = np.where(n < max_exact, n, large) + (rel > 0).astype(np.int32) * half
    return b.astype(np.int32)


def _bias_kernel(band_ref, o_ref):
    x = jnp.broadcast_to(band_ref[0, 0] * LOG2E, (ATTN_QB, BIAS_PAD))
    for variant, shift in enumerate(BIAS_SHIFTS):
        t = pltpu.roll(x, shift, 1, stride=1, stride_axis=0)
        o_ref[0, variant, 0] = t[:, :ATTN_KB]


def _bias_tiles(rel_bias):
    bands = []
    for dil in DILATIONS:
        offs = dil * np.arange(-BAND, BAND + 1, dtype=np.int32)
        bands.append(rel_bias[jnp.asarray(_t5_bucket(offs))].T.astype(F32))
    band = jnp.pad(jnp.stack(bands), ((0, 0), (0, 0), (0, BIAS_PAD - 2 * BAND - 1)), constant_values=NEG_INF)
    return pl.pallas_call(
        _bias_kernel,
        grid=(len(DILATIONS), ATTN_HEADS),
        in_specs=[pl.BlockSpec((1, 1, 1, BIAS_PAD), lambda p, h: (p, h, 0, 0))],
        out_specs=pl.BlockSpec((1, 3, 1, ATTN_QB, ATTN_KB), lambda p, h: (p, 0, h, 0, 0)),
        out_shape=jax.ShapeDtypeStruct((len(DILATIONS), 3, ATTN_HEADS, ATTN_QB, ATTN_KB), F32),
        compiler_params=_cparams(2, 32),
        name="bias_tiles",
    )(band[:, :, None, :])


def _attn_kernel(q_ref, k_ref, v_ref, bias_ref, o_ref, m_ref, l_ref, qd_ref, kd_ref, vd_ref, *, seq, unroll):
    head0 = lax.broadcasted_iota(jnp.int32, (ATTN_QB, LANES), 1) < HEAD_DIM
    zero = jnp.zeros((ATTN_QB, LANES), BF16)
    nsteps = seq // ATTN_QB

    for pi, dil in enumerate(DILATIONS):
        sub_len = seq // dil
        nblk = sub_len // ATTN_QB

        def natural_rows(t, dil=dil, nblk=nblk):
            if dil == 1:
                return pl.ds(pl.multiple_of(t * ATTN_QB, ATTN_QB), ATTN_QB)
            return pl.ds(t // nblk + dil * ATTN_QB * (t % nblk), ATTN_QB, stride=dil)

        def gather(t, carry, natural_rows=natural_rows):
            dst = pl.ds(pl.multiple_of(t * ATTN_QB, ATTN_QB), ATTN_QB)
            src = natural_rows(t)
            qd_ref[dst, :] = (q_ref[0, src, :] * (HEAD_DIM ** -0.5 * LOG2E)).astype(BF16)
            kd_ref[dst, :] = k_ref[0, src, :].astype(BF16)
            vd_ref[dst, :] = v_ref[0, src, :].astype(BF16)
            return carry

        lax.fori_loop(0, nsteps, gather, 0, unroll=unroll)

        def step(t, carry, pi=pi, sub_len=sub_len, nblk=nblk, natural_rows=natural_rows):
            i = t % nblk
            kstart = jnp.clip(i * ATTN_QB - BAND, 0, sub_len - ATTN_KB)
            variant = jnp.where(i == 0, 0, jnp.where(i == nblk - 1, 2, 1))
            krows = pl.ds(pl.multiple_of((t - i) * ATTN_QB + kstart, BAND), ATTN_KB)
            q = qd_ref[pl.ds(pl.multiple_of(t * ATTN_QB, ATTN_QB), ATTN_QB), :]
            k = kd_ref[krows, :]
            v = vd_ref[krows, :]
            pv, den, top = [], [], []
            for h in range(2):
                qh = jnp.where(head0, q, zero) if h == 0 else jnp.where(head0, zero, q)
                s = lax.dot_general(qh, k, (((1,), (1,)), ((), ())), preferred_element_type=F32)
                s = s + bias_ref[pi, variant, h]
                m = jnp.max(s, axis=-1, keepdims=True)
                p = jnp.exp2(s - m)
                pv.append(jnp.dot(p.astype(BF16), v, preferred_element_type=F32))
                den.append(jnp.sum(p, axis=-1, keepdims=True))
                top.append(m)
            acc = jnp.where(head0, pv[0], pv[1])
            l = jnp.where(head0, den[0], den[1])
            m = jnp.where(head0, top[0], top[1])
            dst = natural_rows(t)
            if pi > 0:
                m_run = m_ref[dst, :]
                m_new = jnp.maximum(m, m_run)
                w_cur = jnp.exp2(m - m_new)
                w_run = jnp.exp2(m_run - m_new)
                acc = w_cur * acc + w_run * o_ref[0, dst, :]
                l = w_cur * l + w_run * l_ref[dst, :]
                m = m_new
            if pi < len(DILATIONS) - 1:
                o_ref[0, dst, :] = acc
                m_ref[dst, :] = m
                l_ref[dst, :] = l
            else:
                o_ref[0, dst, :] = acc / l
            return carry

        lax.fori_loop(0, nsteps, step, 0, unroll=unroll)


def _attn(z3, bias, unroll=8):
    bsz, seq, _ = z3.shape
    assert all(seq // d >= 2 * ATTN_QB and seq % (d * ATTN_QB) == 0 for d in DILATIONS)
    col = lambda off: (lambda g, b: (b, 0, off // LANES + g))
    return pl.pallas_call(
        functools.partial(_attn_kernel, seq=seq, unroll=unroll),
        grid=(ATTN_WIDTH // LANES, bsz),
        in_specs=[pl.BlockSpec((1, seq, LANES), col(Q_OFF)),
                  pl.BlockSpec((1, seq, LANES), col(K_OFF)),
                  pl.BlockSpec((1, seq, LANES), col(V_OFF)),
                  pl.BlockSpec((len(DILATIONS), 3, LANES // HEAD_DIM, ATTN_QB, ATTN_KB),
                               lambda g, b: (0, 0, g, 0, 0))],
        out_specs=pl.BlockSpec((1, seq, LANES), lambda g, b: (b, 0, g)),
        out_shape=jax.ShapeDtypeStruct((bsz, seq, ATTN_WIDTH), F32),
        scratch_shapes=[pltpu.VMEM((seq, LANES), F32)] * 2 + [pltpu.VMEM((seq, LANES), BF16)] * 3,
        compiler_params=_cparams(2, 40),
        name="attn",
    )(z3, z3, z3, bias)


def _dft_mats():
    n = np.arange(FFT_N)
    ang = 2.0 * np.pi * np.outer(n, n) / FFT_N
    c, s = np.cos(ang), np.sin(ang)
    fa = np.concatenate([c, -s], axis=0)
    fb = np.concatenate([s, c], axis=0)
    ang_t = 2.0 * np.pi * np.outer(n, n) / (FFT_N * FFT_N)
    tw_c = np.broadcast_to(np.cos(ang_t)[:, :, None], (FFT_N, FFT_N, LANES))
    tw_s = np.broadcast_to(np.sin(ang_t)[:, :, None], (FFT_N, FFT_N, LANES))
    eye = np.eye(LANES // HEAD_DIM)
    scale = (FFT_N * FFT_N * HEAD_DIM) ** -0.5
    cc = np.kron(eye, c) * scale
    sc = np.kron(eye, s) * scale
    as_bf = lambda a: jnp.asarray(a, dtype=F32).astype(BF16)
    return (as_bf(fa), as_bf(fb), jnp.asarray(tw_c, F32), jnp.asarray(tw_s, F32), as_bf(cc), as_bf(sc))


def _fnet_kernel(x_ref, fa_ref, fb_ref, twc_ref, tws_ref, cc_ref, sc_ref, w_ref, o_ref,
                 bre_ref, bim_ref, zre_ref, zim_ref, *, epi_rows):
    def stage_a(s2, carry):
        x = x_ref[0, pl.ds(s2, FFT_N, stride=FFT_N), :].astype(BF16)
        a = jnp.dot(fa_ref[...], x, preferred_element_type=F32)
        ar, ai = a[:FFT_N], a[FFT_N:]
        c, s = twc_ref[s2], tws_ref[s2]
        dst = pl.ds(pl.multiple_of(s2 * FFT_N, FFT_N), FFT_N)
        bre_ref[dst, :] = ar * c + ai * s
        bim_ref[dst, :] = ai * c - ar * s
        return carry

    lax.fori_loop(0, FFT_N, stage_a, 0)

    def stage_b(k1, carry):
        src = pl.ds(k1, FFT_N, stride=FFT_N)
        z = (jnp.dot(fa_ref[...], bre_ref[src, :].astype(BF16), preferred_element_type=F32)
             + jnp.dot(fb_ref[...], bim_ref[src, :].astype(BF16), preferred_element_type=F32))
        zre_ref[src, :] = z[:FFT_N]
        zim_ref[src, :] = z[FFT_N:]
        return carry

    lax.fori_loop(0, FFT_N, stage_b, 0)

    for e in range(x_ref.shape[1] // epi_rows):
        rows = slice(e * epi_rows, (e + 1) * epi_rows)
        f = (jnp.dot(zre_ref[rows, :].astype(BF16), cc_ref[...], preferred_element_type=F32)
             + jnp.dot(zim_ref[rows, :].astype(BF16), sc_ref[...], preferred_element_type=F32))
        o_ref[0, rows, :] = jnp.dot(f.astype(BF16), w_ref[0], preferred_element_type=F32)


def _fnet(z3, mats, w_pairs):
    bsz, seq, _ = z3.shape
    assert seq == FFT_N * FFT_N
    fa, fb, tw_c, tw_s, cc, sc = mats
    const = lambda shape: pl.BlockSpec(shape, lambda b, j: (0,) * len(shape))
    return pl.pallas_call(
        functools.partial(_fnet_kernel, epi_rows=512),
        grid=(bsz, FNET_WIDTH // LANES),
        in_specs=[pl.BlockSpec((1, seq, LANES), lambda b, j: (b, 0, C_OFF // LANES + j)),
                  const((2 * FFT_N, FFT_N)), const((2 * FFT_N, FFT_N)),
                  const((FFT_N, FFT_N, LANES)), const((FFT_N, FFT_N, LANES)),
                  const((LANES, LANES)), const((LANES, LANES)),
                  pl.BlockSpec((1, LANES, LANES), lambda b, j: (j, 0, 0))],
        out_specs=pl.BlockSpec((1, seq, LANES), lambda b, j: (b, 0, j)),
        out_shape=jax.ShapeDtypeStruct((bsz, seq, FNET_WIDTH), F32),
        scratch_shapes=[pltpu.VMEM((seq, LANES), F32)] * 4,
        compiler_params=_cparams(2, 40),
        name="fnet",
    )(z3, fa, fb, tw_c, tw_s, cc, sc, w_pairs)


def _out_proj_kernel(a_ref, b_ref, c_ref, g_ref, w_ref, x_ref, o_ref, mix_ref):
    j = pl.program_id(1)

    @pl.when(j == 0)
    def _():
        o1, o2 = GMLP_WIDTH, GMLP_WIDTH + ATTN_WIDTH
        mix_ref[:, :o1] = _rms(a_ref[...].astype(F32), g_ref[:, :o1]).astype(BF16)
        mix_ref[:, o1:o2] = _rms(b_ref[...], g_ref[:, o1:o2]).astype(BF16)
        mix_ref[:, o2:] = _rms(c_ref[...], g_ref[:, o2:]).astype(BF16)

    o_ref[...] = x_ref[...] + jnp.dot(mix_ref[...], w_ref[...], preferred_element_type=F32)


def _out_proj(a_out, b_out, c_out, gain, w, x2, tm=1024, tn=512):
    t = x2.shape[0]
    return pl.pallas_call(
        _out_proj_kernel,
        grid=(t // tm, D_MODEL // tn),
        in_specs=[pl.BlockSpec((tm, GMLP_WIDTH), lambda i, j: (i, 0)),
                  pl.BlockSpec((tm, ATTN_WIDTH), lambda i, j: (i, 0)),
                  pl.BlockSpec((tm, FNET_WIDTH), lambda i, j: (i, 0)),
                  pl.BlockSpec((1, D_MODEL), lambda i, j: (0, 0)),
                  pl.BlockSpec((D_MODEL, tn), lambda i, j: (0, j)),
                  pl.BlockSpec((tm, tn), lambda i, j: (i, j))],
        out_specs=pl.BlockSpec((tm, tn), lambda i, j: (i, j)),
        out_shape=jax.ShapeDtypeStruct((t, D_MODEL), F32),
        scratch_shapes=[pltpu.VMEM((tm, D_MODEL), BF16)],
        compiler_params=_cparams(2, 56),
        name="out_proj",
    )(a_out, b_out, c_out, gain, w, x2)


def _ffn_up_kernel(x_ref, xp_ref, xn_ref, g_ref, wg_ref, wu_ref, cwg_ref, cwu_ref, cbg_ref, cbu_ref,
                   o_ref, hn_ref, halo_ref, *, tiles_per_seq):
    i = pl.program_id(0)
    j = pl.program_id(1)
    tm = x_ref.shape[0]

    @pl.when(j == 0)
    def _():
        hn_ref[...] = _rms(x_ref[...], g_ref[...]).astype(BF16)
        halo_ref[:HALO] = _rms(xp_ref[...], g_ref[...]).astype(BF16)
        halo_ref[HALO:] = _rms(xn_ref[...], g_ref[...]).astype(BF16)

    has_prev = (i % tiles_per_seq != 0).astype(F32)
    has_next = (i % tiles_per_seq != tiles_per_seq - 1).astype(F32)
    row = lax.broadcasted_iota(jnp.int32, (tm, 1), 0)

    def branch(w_ref, cw_ref, cb_ref):
        h = jnp.dot(hn_ref[...], w_ref[...], preferred_element_type=F32)
        hh = jnp.dot(halo_ref[...], w_ref[...], preferred_element_type=F32)
        prev_row = hh[HALO - 1:HALO] * has_prev
        next_row = hh[HALO:HALO + 1] * has_next
        up = jnp.where(row == 0, prev_row, pltpu.roll(h, 1, axis=0))
        dn = jnp.where(row == tm - 1, next_row, pltpu.roll(h, tm - 1, axis=0))
        return up * cw_ref[0:1] + h * cw_ref[1:2] + dn * cw_ref[2:3] + cb_ref[...]

    gate = branch(wg_ref, cwg_ref, cbg_ref)
    up = branch(wu_ref, cwu_ref, cbu_ref)
    o_ref[...] = (gate * jax.nn.sigmoid(gate) * up).astype(BF16)


def _ffn_up(x2, gain, w, conv_w, conv_b, seq, tm=1024, tn=512):
    t = x2.shape[0]
    nj = D_FF // tn
    hb = tm // HALO
    last = t // HALO - 1
    return pl.pallas_call(
        functools.partial(_ffn_up_kernel, tiles_per_seq=seq // tm),
        grid=(t // tm, nj),
        in_specs=[pl.BlockSpec((tm, D_MODEL), lambda i, j: (i, 0)),
                  pl.BlockSpec((HALO, D_MODEL), lambda i, j: (jnp.maximum(i * hb - 1, 0), 0)),
                  pl.BlockSpec((HALO, D_MODEL), lambda i, j: (jnp.minimum((i + 1) * hb, last), 0)),
                  pl.BlockSpec((1, D_MODEL), lambda i, j: (0, 0)),
                  pl.BlockSpec((D_MODEL, tn), lambda i, j: (0, j)),
                  pl.BlockSpec((D_MODEL, tn), lambda i, j: (0, j + nj)),
                  pl.BlockSpec((3, tn), lambda i, j: (0, j)),
                  pl.BlockSpec((3, tn), lambda i, j: (0, j + nj)),
                  pl.BlockSpec((1, tn), lambda i, j: (0, j)),
                  pl.BlockSpec((1, tn), lambda i, j: (0, j + nj))],
        out_specs=pl.BlockSpec((tm, tn), lambda i, j: (i, j)),
        out_shape=jax.ShapeDtypeStruct((t, D_FF), BF16),
        scratch_shapes=[pltpu.VMEM((tm, D_MODEL), BF16), pltpu.VMEM((2 * HALO, D_MODEL), BF16)],
        compiler_params=_cparams(2, 56),
        name="ffn_up",
    )(x2, x2, x2, gain, w, w, conv_w, conv_w, conv_b, conv_b)


def _ffn_down_kernel(a_ref, w_ref, x_ref, o_ref):
    o_ref[...] = x_ref[...] + jnp.dot(a_ref[...], w_ref[...], preferred_element_type=F32)


def _ffn_down(act, w, x2, tm=1024, tn=512):
    t = x2.shape[0]
    return pl.pallas_call(
        _ffn_down_kernel,
        grid=(t // tm, D_MODEL // tn),
        in_specs=[pl.BlockSpec((tm, D_FF), lambda i, j: (i, 0)),
                  pl.BlockSpec((D_FF, tn), lambda i, j: (0, j)),
                  pl.BlockSpec((tm, tn), lambda i, j: (i, j))],
        out_specs=pl.BlockSpec((tm, tn), lambda i, j: (i, j)),
        out_shape=jax.ShapeDtypeStruct((t, D_MODEL), F32),
        compiler_params=_cparams(2, 56),
        name="ffn_down",
    )(act, w, x2)


def _final_norm_kernel(x_ref, g_ref, o_ref):
    o_ref[...] = _rms(x_ref[...], g_ref[...])


def _final_norm(x2, gain, tm=512):
    t = x2.shape[0]
    return pl.pallas_call(
        _final_norm_kernel,
        grid=(t // tm,),
        in_specs=[pl.BlockSpec((tm, D_MODEL), lambda i: (i, 0)),
                  pl.BlockSpec((1, D_MODEL), lambda i: (0, 0))],
        out_specs=pl.BlockSpec((tm, D_MODEL), lambda i: (i, 0)),
        out_shape=jax.ShapeDtypeStruct((t, D_MODEL), F32),
        compiler_params=_cparams(1, 32),
        name="final_norm",
    )(x2, gain)


def kernel(x, w_in, gmlp_ws, gmlp_b, fnet_w, mix_gain, w_out, norm_mix, norm_ffn, ffn_up, ffn_conv_w,
           ffn_conv_b, ffn_down, rel_bias, final_norm):
    bsz, seq, _ = x.shape
    depth = w_in.shape[0]
    t = bsz * seq
    x2 = x.reshape(t, D_MODEL)
    mats = _dft_mats()
    bias = _bias_tiles(rel_bias)
    eye = jnp.eye(LANES // HEAD_DIM, dtype=F32)
    for l in range(depth):
        z = _proj_in(x2, norm_mix[l][None], w_in[l].astype(BF16))
        ws_pairs = gmlp_ws[l].astype(BF16).reshape(GMLP_HEADS // 2, 2 * CHUNK, CHUNK)
        b_tile = jnp.repeat(gmlp_b[l].T, HEAD_DIM, axis=1)
        a_out = _gmlp(z, ws_pairs, b_tile)
        z3 = z.reshape(bsz, seq, IN_WIDTH)
        b_out = _attn(z3, bias).reshape(t, ATTN_WIDTH)
        w_pairs = jnp.einsum("gh,pgce->pgche", eye, fnet_w[l].reshape(-1, 2, HEAD_DIM, HEAD_DIM))
        w_pairs = w_pairs.reshape(-1, LANES, LANES).astype(BF16)
        c_out = _fnet(z3, mats, w_pairs).reshape(t, FNET_WIDTH)
        x2 = _out_proj(a_out, b_out, c_out, mix_gain[l][None], w_out[l].astype(BF16), x2)
        act = _ffn_up(x2, norm_ffn[l][None], ffn_up[l].astype(BF16), ffn_conv_w[l], ffn_conv_b[l][None], seq)
        x2 = _ffn_down(act, ffn_down[l].astype(BF16), x2)
    return _final_norm(x2, final_norm[None]).reshape(bsz, seq, D_MODEL)
```

```python
import functools

import numpy as np
import jax
import jax.numpy as jnp
from jax import lax
from jax.experimental import pallas as pl
from jax.experimental.pallas import tpu as pltpu

D_MODEL = 2048
HEAD_DIM = 64
GMLP_WIDTH = 512
ATTN_WIDTH = 1024
FNET_WIDTH = 512
GMLP_HEADS = GMLP_WIDTH // HEAD_DIM
ATTN_HEADS = ATTN_WIDTH // HEAD_DIM
FNET_GROUPS = FNET_WIDTH // HEAD_DIM
CHUNK = 128
DILATIONS = (1, 4, 16)
REL_BUCKETS = 32
REL_MAX_DISTANCE = 1024
D_FF = 5632
EPS = 1e-6
IN_WIDTH = 2 * GMLP_WIDTH + 3 * ATTN_WIDTH + FNET_WIDTH
NEG_INF = -1e30
LOG2E = 1.4426950408889634

Q_OFF = 2 * GMLP_WIDTH
K_OFF = Q_OFF + ATTN_WIDTH
V_OFF = K_OFF + ATTN_WIDTH
C_OFF = V_OFF + ATTN_WIDTH

LANES = 128
MXU_N = 256
BAND = 64
ATTN_QB = 128
ATTN_KB = ATTN_QB + 2 * BAND
FFT_N = 64
HALO = 16
BIAS_PAD = 512
BIAS_SHIFTS = (BIAS_PAD - BAND, 0, BAND)

F32 = jnp.float32
BF16 = jnp.bfloat16
MIB = 1024 * 1024


def _cparams(n_axes, vmem_mib):
    return pltpu.CompilerParams(dimension_semantics=("arbitrary",) * n_axes,
                                vmem_limit_bytes=vmem_mib * MIB)


def _rms(x, g):
    return x * lax.rsqrt(jnp.mean(x * x, axis=-1, keepdims=True) + EPS) * g


def _proj_in_kernel(x_ref, g_ref, w_ref, o_ref, xn_ref, *, gelu_tiles):
    j = pl.program_id(1)

    @pl.when(j == 0)
    def _():
        xn_ref[...] = _rms(x_ref[...], g_ref[...]).astype(BF16)

    acc = jnp.dot(xn_ref[...], w_ref[...], preferred_element_type=F32)

    @pl.when(j < gelu_tiles)
    def _():
        o_ref[...] = 0.5 * acc * (1.0 + lax.erf(acc * np.float32(2.0 ** -0.5)))

    @pl.when(j >= gelu_tiles)
    def _():
        o_ref[...] = acc


def _proj_in(x2, gain, w, layer, tm=1024, tn=512):
    t = x2.shape[0]
    return pl.pallas_call(
        functools.partial(_proj_in_kernel, gelu_tiles=Q_OFF // tn),
        grid=(t // tm, IN_WIDTH // tn),
        in_specs=[pl.BlockSpec((tm, D_MODEL), lambda i, j: (i, 0)),
                  pl.BlockSpec((1, D_MODEL), lambda i, j: (0, 0)),
                  pl.BlockSpec((None, D_MODEL, tn), lambda i, j: (layer, 0, j))],
        out_specs=pl.BlockSpec((tm, tn), lambda i, j: (i, j)),
        out_shape=jax.ShapeDtypeStruct((t, IN_WIDTH), F32),
        scratch_shapes=[pltpu.VMEM((tm, D_MODEL), BF16)],
        compiler_params=_cparams(2, 56),
        name="proj_in",
    )(x2, gain, w)


def _gmlp_kernel(u_ref, v_ref, ws_ref, b_ref, o_ref, *, chunks):
    lane = lax.broadcasted_iota(jnp.int32, (CHUNK, 2 * HEAD_DIM), 1)
    for c in range(chunks):
        rows = slice(c * CHUNK, (c + 1) * CHUNK)
        for p in range(GMLP_HEADS // 2):
            cols = slice(p * 2 * HEAD_DIM, (p + 1) * 2 * HEAD_DIM)
            both = jnp.dot(ws_ref[p], v_ref[rows, cols].astype(BF16), preferred_element_type=F32)
            gate = jnp.where(lane < HEAD_DIM, both[:CHUNK], both[CHUNK:]) + b_ref[:, cols]
            o_ref[rows, cols] = (u_ref[rows, cols] * gate).astype(BF16)


def _gmlp(z, ws_pairs, bias_tile, chunks=4):
    t = z.shape[0]
    tm = chunks * CHUNK
    return pl.pallas_call(
        functools.partial(_gmlp_kernel, chunks=chunks),
        grid=(t // tm,),
        in_specs=[pl.BlockSpec((tm, GMLP_WIDTH), lambda i: (i, 0)),
                  pl.BlockSpec((tm, GMLP_WIDTH), lambda i: (i, 1)),
                  pl.BlockSpec((GMLP_HEADS // 2, 2 * CHUNK, CHUNK), lambda i: (0, 0, 0)),
                  pl.BlockSpec((CHUNK, GMLP_WIDTH), lambda i: (0, 0))],
        out_specs=pl.BlockSpec((tm, GMLP_WIDTH), lambda i: (i, 0)),
        out_shape=jax.ShapeDtypeStruct((t, GMLP_WIDTH), BF16),
        compiler_params=_cparams(1, 32),
        name="gmlp",
    )(z, z, ws_pairs, bias_tile)


def _t5_bucket(rel):
    half = REL_BUCKETS // 2
    max_exact = half // 2
    n = np.abs(rel)
    nl = np.maximum(n, max_exact).astype(np.float32)
    large = max_exact + (np.log(nl / max_exact) / np.log(REL_MAX_DISTANCE / max_exact)
                         * (half - max_exact)).astype(np.int32)
    large = np.minimum(large, half - 1)
    b = np.where(n < max_exact, n, large) + (rel > 0).astype(np.int32) * half
    return b.astype(np.int32)


def _bias_kernel(band_ref, o_ref):
    x = jnp.broadcast_to(band_ref[0, 0] * LOG2E, (ATTN_QB, BIAS_PAD))
    for variant, shift in enumerate(BIAS_SHIFTS):
        t = pltpu.roll(x, shift, 1, stride=1, stride_axis=0)
        o_ref[0, variant, 0] = t[:, :ATTN_KB]


def _bias_tiles(rel_bias):
    bands = []
    for dil in DILATIONS:
        offs = dil * np.arange(-BAND, BAND + 1, dtype=np.int32)
        bands.append(rel_bias[jnp.asarray(_t5_bucket(offs))].T.astype(F32))
    band = jnp.pad(jnp.stack(bands), ((0, 0), (0, 0), (0, BIAS_PAD - 2 * BAND - 1)), constant_values=NEG_INF)
    return pl.pallas_call(
        _bias_kernel,
        grid=(len(DILATIONS), ATTN_HEADS),
        in_specs=[pl.BlockSpec((1, 1, 1, BIAS_PAD), lambda p, h: (p, h, 0, 0))],
        out_specs=pl.BlockSpec((1, 3, 1, ATTN_QB, ATTN_KB), lambda p, h: (p, 0, h, 0, 0)),
        out_shape=jax.ShapeDtypeStruct((len(DILATIONS), 3, ATTN_HEADS, ATTN_QB, ATTN_KB), F32),
        compiler_params=_cparams(2, 32),
        name="bias_tiles",
    )(band[:, :, None, :])


def _attn_kernel(q_ref, k_ref, v_ref, bias_ref, o_ref, m_ref, l_ref, qd_ref, kd_ref, vd_ref, *, seq, unroll):
    head0 = lax.broadcasted_iota(jnp.int32, (ATTN_QB, LANES), 1) < HEAD_DIM
    zero = jnp.zeros((ATTN_QB, LANES), BF16)
    nsteps = seq // ATTN_QB

    for pi, dil in enumerate(DILATIONS):
        sub_len = seq // dil
        nblk = sub_len // ATTN_QB

        def natural_rows(t, dil=dil, nblk=nblk):
            if dil == 1:
                return pl.ds(pl.multiple_of(t * ATTN_QB, ATTN_QB), ATTN_QB)
            return pl.ds(t // nblk + dil * ATTN_QB * (t % nblk), ATTN_QB, stride=dil)

        def gather(t, carry, natural_rows=natural_rows):
            dst = pl.ds(pl.multiple_of(t * ATTN_QB, ATTN_QB), ATTN_QB)
            src = natural_rows(t)
            qd_ref[dst, :] = (q_ref[0, src, :] * (HEAD_DIM ** -0.5 * LOG2E)).astype(BF16)
            kd_ref[dst, :] = k_ref[0, src, :].astype(BF16)
            vd_ref[dst, :] = v_ref[0, src, :].astype(BF16)
            return carry

        lax.fori_loop(0, nsteps, gather, 0, unroll=unroll)

        def step(t, carry, pi=pi, sub_len=sub_len, nblk=nblk, natural_rows=natural_rows):
            i = t % nblk
            kstart = jnp.clip(i * ATTN_QB - BAND, 0, sub_len - ATTN_KB)
            variant = jnp.where(i == 0, 0, jnp.where(i == nblk - 1, 2, 1))
            krows = pl.ds(pl.multiple_of((t - i) * ATTN_QB + kstart, BAND), ATTN_KB)
            q = qd_ref[pl.ds(pl.multiple_of(t * ATTN_QB, ATTN_QB), ATTN_QB), :]
            k = kd_ref[krows, :]
            v = vd_ref[krows, :]
            pv, den, top = [], [], []
            for h in range(2):
                qh = jnp.where(head0, q, zero) if h == 0 else jnp.where(head0, zero, q)
                s = lax.dot_general(qh, k, (((1,), (1,)), ((), ())), preferred_element_type=F32)
                s = s + bias_ref[pi, variant, h]
                m = jnp.max(s, axis=-1, keepdims=True)
                p = jnp.exp2(s - m)
                pv.append(jnp.dot(p.astype(BF16), v, preferred_element_type=F32))
                den.append(jnp.sum(p, axis=-1, keepdims=True))
                top.append(m)
            acc = jnp.where(head0, pv[0], pv[1])
            l = jnp.where(head0, den[0], den[1])
            m = jnp.where(head0, top[0], top[1])
            dst = natural_rows(t)
            if pi > 0:
                m_run = m_ref[dst, :]
                m_new = jnp.maximum(m, m_run)
                w_cur = jnp.exp2(m - m_new)
                w_run = jnp.exp2(m_run - m_new)
                acc = w_cur * acc + w_run * o_ref[0, dst, :]
                l = w_cur * l + w_run * l_ref[dst, :]
                m = m_new
            if pi < len(DILATIONS) - 1:
                o_ref[0, dst, :] = acc
                m_ref[dst, :] = m
                l_ref[dst, :] = l
            else:
                o_ref[0, dst, :] = acc / l
            return carry

        lax.fori_loop(0, nsteps, step, 0, unroll=unroll)


def _attn(z3, bias, unroll=8):
    bsz, seq, _ = z3.shape
    assert all(seq // d >= 2 * ATTN_QB and seq % (d * ATTN_QB) == 0 for d in DILATIONS)
    col = lambda off: (lambda g, b: (b, 0, off // LANES + g))
    return pl.pallas_call(
        functools.partial(_attn_kernel, seq=seq, unroll=unroll),
        grid=(ATTN_WIDTH // LANES, bsz),
        in_specs=[pl.BlockSpec((1, seq, LANES), col(Q_OFF)),
                  pl.BlockSpec((1, seq, LANES), col(K_OFF)),
                  pl.BlockSpec((1, seq, LANES), col(V_OFF)),
                  pl.BlockSpec((len(DILATIONS), 3, LANES // HEAD_DIM, ATTN_QB, ATTN_KB),
                               lambda g, b: (0, 0, g, 0, 0))],
        out_specs=pl.BlockSpec((1, seq, LANES), lambda g, b: (b, 0, g)),
        out_shape=jax.ShapeDtypeStruct((bsz, seq, ATTN_WIDTH), F32),
        scratch_shapes=[pltpu.VMEM((seq, LANES), F32)] * 2 + [pltpu.VMEM((seq, LANES), BF16)] * 3,
        compiler_params=_cparams(2, 40),
        name="attn",
    )(z3, z3, z3, bias)


def _dft_mats():
    n = np.arange(FFT_N)
    ang = 2.0 * np.pi * np.outer(n, n) / FFT_N
    c, s = np.cos(ang), np.sin(ang)
    fa = np.concatenate([c, -s], axis=0)
    fb = np.concatenate([s, c], axis=0)
    ang_t = 2.0 * np.pi * np.outer(n, n) / (FFT_N * FFT_N)
    tw_c = np.broadcast_to(np.cos(ang_t)[:, :, None], (FFT_N, FFT_N, LANES))
    tw_s = np.broadcast_to(np.sin(ang_t)[:, :, None], (FFT_N, FFT_N, LANES))
    eye = np.eye(LANES // HEAD_DIM)
    scale = (FFT_N * FFT_N * HEAD_DIM) ** -0.5
    cc = np.kron(eye, c) * scale
    sc = np.kron(eye, s) * scale
    as_bf = lambda a: jnp.asarray(a, dtype=F32).astype(BF16)
    return (as_bf(fa), as_bf(fb), jnp.asarray(tw_c, F32), jnp.asarray(tw_s, F32), as_bf(cc), as_bf(sc))


def _fnet_kernel(x_ref, fa_ref, fb_ref, twc_ref, tws_ref, cc_ref, sc_ref, w_ref, o_ref,
                 bre_ref, bim_ref, zre_ref, zim_ref, *, epi_rows, unroll):
    def stage_a(s2, carry):
        x = x_ref[0, pl.ds(s2, FFT_N, stride=FFT_N), :].astype(BF16)
        a = jnp.dot(fa_ref[...], x, preferred_element_type=F32)
        ar, ai = a[:FFT_N], a[FFT_N:]
        c, s = twc_ref[s2], tws_ref[s2]
        dst = pl.ds(pl.multiple_of(s2 * FFT_N, FFT_N), FFT_N)
        bre_ref[dst, :] = ar * c + ai * s
        bim_ref[dst, :] = ai * c - ar * s
        return carry

    lax.fori_loop(0, FFT_N, stage_a, 0, unroll=unroll)

    def stage_b(k1, carry):
        src = pl.ds(k1, FFT_N, stride=FFT_N)
        z = (jnp.dot(fa_ref[...], bre_ref[src, :].astype(BF16), preferred_element_type=F32)
             + jnp.dot(fb_ref[...], bim_ref[src, :].astype(BF16), preferred_element_type=F32))
        zre_ref[src, :] = z[:FFT_N]
        zim_ref[src, :] = z[FFT_N:]
        return carry

    lax.fori_loop(0, FFT_N, stage_b, 0, unroll=unroll)

    for e in range(x_ref.shape[1] // epi_rows):
        rows = slice(e * epi_rows, (e + 1) * epi_rows)
        f = (jnp.dot(zre_ref[rows, :].astype(BF16), cc_ref[...], preferred_element_type=F32)
             + jnp.dot(zim_ref[rows, :].astype(BF16), sc_ref[...], preferred_element_type=F32))
        o_ref[0, rows, :] = jnp.dot(f.astype(BF16), w_ref[0], preferred_element_type=F32)


def _fnet(z3, mats, w_pairs):
    bsz, seq, _ = z3.shape
    assert seq == FFT_N * FFT_N
    fa, fb, tw_c, tw_s, cc, sc = mats
    const = lambda shape: pl.BlockSpec(shape, lambda b, j: (0,) * len(shape))
    return pl.pallas_call(
        functools.partial(_fnet_kernel, epi_rows=512, unroll=8),
        grid=(bsz, FNET_WIDTH // LANES),
        in_specs=[pl.BlockSpec((1, seq, LANES), lambda b, j: (b, 0, C_OFF // LANES + j)),
                  const((2 * FFT_N, FFT_N)), const((2 * FFT_N, FFT_N)),
                  const((FFT_N, FFT_N, LANES)), const((FFT_N, FFT_N, LANES)),
                  const((LANES, LANES)), const((LANES, LANES)),
                  pl.BlockSpec((1, LANES, LANES), lambda b, j: (j, 0, 0))],
        out_specs=pl.BlockSpec((1, seq, LANES), lambda b, j: (b, 0, j)),
        out_shape=jax.ShapeDtypeStruct((bsz, seq, FNET_WIDTH), F32),
        scratch_shapes=[pltpu.VMEM((seq, LANES), F32)] * 4,
        compiler_params=_cparams(2, 40),
        name="fnet",
    )(z3, fa, fb, tw_c, tw_s, cc, sc, w_pairs)


def _out_proj_kernel(a_ref, b_ref, c_ref, g_ref, w_ref, x_ref, o_ref, mix_ref):
    j = pl.program_id(1)

    @pl.when(j == 0)
    def _():
        o1, o2 = GMLP_WIDTH, GMLP_WIDTH + ATTN_WIDTH
        mix_ref[:, :o1] = _rms(a_ref[...].astype(F32), g_ref[:, :o1]).astype(BF16)
        mix_ref[:, o1:o2] = _rms(b_ref[...], g_ref[:, o1:o2]).astype(BF16)
        mix_ref[:, o2:] = _rms(c_ref[...], g_ref[:, o2:]).astype(BF16)

    o_ref[...] = x_ref[...] + jnp.dot(mix_ref[...], w_ref[...], preferred_element_type=F32)


def _out_proj(a_out, b_out, c_out, gain, w, layer, x2, tm=1024, tn=512):
    t = x2.shape[0]
    return pl.pallas_call(
        _out_proj_kernel,
        grid=(t // tm, D_MODEL // tn),
        in_specs=[pl.BlockSpec((tm, GMLP_WIDTH), lambda i, j: (i, 0)),
                  pl.BlockSpec((tm, ATTN_WIDTH), lambda i, j: (i, 0)),
                  pl.BlockSpec((tm, FNET_WIDTH), lambda i, j: (i, 0)),
                  pl.BlockSpec((1, D_MODEL), lambda i, j: (0, 0)),
                  pl.BlockSpec((None, D_MODEL, tn), lambda i, j: (layer, 0, j)),
                  pl.BlockSpec((tm, tn), lambda i, j: (i, j))],
        out_specs=pl.BlockSpec((tm, tn), lambda i, j: (i, j)),
        out_shape=jax.ShapeDtypeStruct((t, D_MODEL), F32),
        scratch_shapes=[pltpu.VMEM((tm, D_MODEL), BF16)],
        compiler_params=_cparams(2, 56),
        name="out_proj",
    )(a_out, b_out, c_out, gain, w, x2)


def _ffn_up_kernel(x_ref, xp_ref, xn_ref, g_ref, wg_ref, wu_ref, cwg_ref, cwu_ref, cbg_ref, cbu_ref,
                   o_ref, hn_ref, *, tiles_per_seq):
    i = pl.program_id(0)
    j = pl.program_id(1)
    tm = x_ref.shape[0]
    ext = tm + 2 * HALO

    @pl.when(j == 0)
    def _():
        has_prev = (i % tiles_per_seq != 0).astype(F32)
        has_next = (i % tiles_per_seq != tiles_per_seq - 1).astype(F32)
        hn_ref[:HALO] = (_rms(xp_ref[...], g_ref[...]) * has_prev).astype(BF16)
        hn_ref[HALO:HALO + tm] = _rms(x_ref[...], g_ref[...]).astype(BF16)
        hn_ref[HALO + tm:] = (_rms(xn_ref[...], g_ref[...]) * has_next).astype(BF16)

    def branch(w_ref, cw_ref, cb_ref):
        h = jnp.dot(hn_ref[...], w_ref[...], preferred_element_type=F32)
        up = pltpu.roll(h, 1, axis=0)[HALO:HALO + tm]
        dn = pltpu.roll(h, ext - 1, axis=0)[HALO:HALO + tm]
        return up * cw_ref[0:1] + h[HALO:HALO + tm] * cw_ref[1:2] + dn * cw_ref[2:3] + cb_ref[...]

    gate = branch(wg_ref, cwg_ref, cbg_ref)
    up = branch(wu_ref, cwu_ref, cbu_ref)
    o_ref[...] = (gate * jax.nn.sigmoid(gate) * up).astype(BF16)


def _ffn_up(x2, gain, w, layer, conv_w, conv_b, seq, tm=1024, tn=512):
    t = x2.shape[0]
    nj = D_FF // tn
    hb = tm // HALO
    last = t // HALO - 1
    return pl.pallas_call(
        functools.partial(_ffn_up_kernel, tiles_per_seq=seq // tm),
        grid=(t // tm, nj),
        in_specs=[pl.BlockSpec((tm, D_MODEL), lambda i, j: (i, 0)),
                  pl.BlockSpec((HALO, D_MODEL), lambda i, j: (jnp.maximum(i * hb - 1, 0), 0)),
                  pl.BlockSpec((HALO, D_MODEL), lambda i, j: (jnp.minimum((i + 1) * hb, last), 0)),
                  pl.BlockSpec((1, D_MODEL), lambda i, j: (0, 0)),
                  pl.BlockSpec((None, D_MODEL, tn), lambda i, j: (layer, 0, j)),
                  pl.BlockSpec((None, D_MODEL, tn), lambda i, j: (layer, 0, j + nj)),
                  pl.BlockSpec((3, tn), lambda i, j: (0, j)),
                  pl.BlockSpec((3, tn), lambda i, j: (0, j + nj)),
                  pl.BlockSpec((1, tn), lambda i, j: (0, j)),
                  pl.BlockSpec((1, tn), lambda i, j: (0, j + nj))],
        out_specs=pl.BlockSpec((tm, tn), lambda i, j: (i, j)),
        out_shape=jax.ShapeDtypeStruct((t, D_FF), BF16),
        scratch_shapes=[pltpu.VMEM((tm + 2 * HALO, D_MODEL), BF16)],
        compiler_params=_cparams(2, 56),
        name="ffn_up",
    )(x2, x2, x2, gain, w, w, conv_w, conv_w, conv_b, conv_b)


def _ffn_down_kernel(a_ref, w_ref, x_ref, o_ref):
    o_ref[...] = x_ref[...] + jnp.dot(a_ref[...], w_ref[...], preferred_element_type=F32)


def _ffn_down(act, w, layer, x2, tm=1024, tn=512):
    t = x2.shape[0]
    return pl.pallas_call(
        _ffn_down_kernel,
        grid=(t // tm, D_MODEL // tn),
        in_specs=[pl.BlockSpec((tm, D_FF), lambda i, j: (i, 0)),
                  pl.BlockSpec((None, D_FF, tn), lambda i, j: (layer, 0, j)),
                  pl.BlockSpec((tm, tn), lambda i, j: (i, j))],
        out_specs=pl.BlockSpec((tm, tn), lambda i, j: (i, j)),
        out_shape=jax.ShapeDtypeStruct((t, D_MODEL), F32),
        compiler_params=_cparams(2, 56),
        name="ffn_down",
    )(act, w, x2)


def _final_norm_kernel(x_ref, g_ref, o_ref):
    o_ref[...] = _rms(x_ref[...], g_ref[...])


def _final_norm(x2, gain, tm=512):
    t = x2.shape[0]
    return pl.pallas_call(
        _final_norm_kernel,
        grid=(t // tm,),
        in_specs=[pl.BlockSpec((tm, D_MODEL), lambda i: (i, 0)),
                  pl.BlockSpec((1, D_MODEL), lambda i: (0, 0))],
        out_specs=pl.BlockSpec((tm, D_MODEL), lambda i: (i, 0)),
        out_shape=jax.ShapeDtypeStruct((t, D_MODEL), F32),
        compiler_params=_cparams(1, 32),
        name="final_norm",
    )(x2, gain)


def kernel(x, w_in, gmlp_ws, gmlp_b, fnet_w, mix_gain, w_out, norm_mix, norm_ffn, ffn_up, ffn_conv_w,
           ffn_conv_b, ffn_down, rel_bias, final_norm):
    bsz, seq, _ = x.shape
    depth = w_in.shape[0]
    t = bsz * seq
    x2 = x.reshape(t, D_MODEL)
    mats = _dft_mats()
    bias = _bias_tiles(rel_bias)
    eye = jnp.eye(LANES // HEAD_DIM, dtype=F32)
    w_in, w_out, ffn_up, ffn_down = (w.astype(BF16) for w in (w_in, w_out, ffn_up, ffn_down))
    for l in range(depth):
        z = _proj_in(x2, norm_mix[l][None], w_in, l)
        ws_pairs = gmlp_ws[l].astype(BF16).reshape(GMLP_HEADS // 2, 2 * CHUNK, CHUNK)
        b_tile = jnp.repeat(gmlp_b[l].T, HEAD_DIM, axis=1)
        a_out = _gmlp(z, ws_pairs, b_tile)
        z3 = z.reshape(bsz, seq, IN_WIDTH)
        b_out = _attn(z3, bias).reshape(t, ATTN_WIDTH)
        w_pairs = jnp.einsum("gh,pgce->pgche", eye, fnet_w[l].reshape(-1, 2, HEAD_DIM, HEAD_DIM))
        w_pairs = w_pairs.reshape(-1, LANES, LANES).astype(BF16)
        c_out = _fnet(z3, mats, w_pairs).reshape(t, FNET_WIDTH)
        x2 = _out_proj(a_out, b_out, c_out, mix_gain[l][None], w_out, l, x2)
        act = _ffn_up(x2, norm_ffn[l][None], ffn_up, l, ffn_conv_w[l], ffn_conv_b[l][None], seq)
        x2 = _ffn_down(act, ffn_down, l, x2)
    return _final_norm(x2, final_norm[None]).reshape(bsz, seq, D_MODEL)
```

```python
import functools

import numpy as np
import jax
import jax.numpy as jnp
from jax import lax
from jax.experimental import pallas as pl
from jax.experimental.pallas import tpu as pltpu

D_MODEL = 2048
HEAD_DIM = 64
GMLP_WIDTH = 512
ATTN_WIDTH = 1024
FNET_WIDTH = 512
GMLP_HEADS = GMLP_WIDTH // HEAD_DIM
ATTN_HEADS = ATTN_WIDTH // HEAD_DIM
FNET_GROUPS = FNET_WIDTH // HEAD_DIM
CHUNK = 128
DILATIONS = (1, 4, 16)
REL_BUCKETS = 32
REL_MAX_DISTANCE = 1024
D_FF = 5632
EPS = 1e-6
IN_WIDTH = 2 * GMLP_WIDTH + 3 * ATTN_WIDTH + FNET_WIDTH
NEG_INF = -1e30
LOG2E = 1.4426950408889634

Q_OFF = 2 * GMLP_WIDTH
K_OFF = Q_OFF + ATTN_WIDTH
V_OFF = K_OFF + ATTN_WIDTH
C_OFF = V_OFF + ATTN_WIDTH

LANES = 128
MXU_N = 256
BAND = 64
ATTN_QB = 128
ATTN_KB = ATTN_QB + 2 * BAND
FFT_N = 64
HALO = 16
BIAS_PAD = 512
BIAS_SHIFTS = (BIAS_PAD - BAND, 0, BAND)

F32 = jnp.float32
BF16 = jnp.bfloat16
MIB = 1024 * 1024


def _cparams(n_axes, vmem_mib):
    return pltpu.CompilerParams(dimension_semantics=("arbitrary",) * n_axes,
                                vmem_limit_bytes=vmem_mib * MIB)


def _rms(x, g):
    return x * lax.rsqrt(jnp.mean(x * x, axis=-1, keepdims=True) + EPS) * g


def _proj_in_kernel(x_ref, g_ref, w_ref, o_ref, xn_ref, *, gelu_tiles):
    j = pl.program_id(1)

    @pl.when(j == 0)
    def _():
        xn_ref[...] = _rms(x_ref[...], g_ref[...]).astype(BF16)

    acc = jnp.dot(xn_ref[...], w_ref[...], preferred_element_type=F32)

    @pl.when(j < gelu_tiles)
    def _():
        o_ref[...] = 0.5 * acc * (1.0 + lax.erf(acc * np.float32(2.0 ** -0.5)))

    @pl.when(j >= gelu_tiles)
    def _():
        o_ref[...] = acc


def _proj_in(x2, gain, w, layer, tm=1024, tn=512):
    t = x2.shape[0]
    return pl.pallas_call(
        functools.partial(_proj_in_kernel, gelu_tiles=Q_OFF // tn),
        grid=(t // tm, IN_WIDTH // tn),
        in_specs=[pl.BlockSpec((tm, D_MODEL), lambda i, j: (i, 0)),
                  pl.BlockSpec((1, D_MODEL), lambda i, j: (0, 0)),
                  pl.BlockSpec((None, D_MODEL, tn), lambda i, j: (layer, 0, j))],
        out_specs=pl.BlockSpec((tm, tn), lambda i, j: (i, j)),
        out_shape=jax.ShapeDtypeStruct((t, IN_WIDTH), F32),
        scratch_shapes=[pltpu.VMEM((tm, D_MODEL), BF16)],
        compiler_params=_cparams(2, 56),
        name="proj_in",
    )(x2, gain, w)


def _gmlp_kernel(u_ref, v_ref, ws_ref, b_ref, o_ref, *, chunks):
    lane = lax.broadcasted_iota(jnp.int32, (CHUNK, 2 * HEAD_DIM), 1)
    for c in range(chunks):
        rows = slice(c * CHUNK, (c + 1) * CHUNK)
        for p in range(GMLP_HEADS // 2):
            cols = slice(p * 2 * HEAD_DIM, (p + 1) * 2 * HEAD_DIM)
            both = jnp.dot(ws_ref[p], v_ref[rows, cols].astype(BF16), preferred_element_type=F32)
            gate = jnp.where(lane < HEAD_DIM, both[:CHUNK], both[CHUNK:]) + b_ref[:, cols]
            o_ref[rows, cols] = (u_ref[rows, cols] * gate).astype(BF16)


def _gmlp(z, ws_pairs, bias_tile, chunks=4):
    t = z.shape[0]
    tm = chunks * CHUNK
    return pl.pallas_call(
        functools.partial(_gmlp_kernel, chunks=chunks),
        grid=(t // tm,),
        in_specs=[pl.BlockSpec((tm, GMLP_WIDTH), lambda i: (i, 0)),
                  pl.BlockSpec((tm, GMLP_WIDTH), lambda i: (i, 1)),
                  pl.BlockSpec((GMLP_HEADS // 2, 2 * CHUNK, CHUNK), lambda i: (0, 0, 0)),
                  pl.BlockSpec((CHUNK, GMLP_WIDTH), lambda i: (0, 0))],
        out_specs=pl.BlockSpec((tm, GMLP_WIDTH), lambda i: (i, 0)),
        out_shape=jax.ShapeDtypeStruct((t, GMLP_WIDTH), BF16),
        compiler_params=_cparams(1, 32),
        name="gmlp",
    )(z, z, ws_pairs, bias_tile)


def _t5_bucket(rel):
    half = REL_BUCKETS // 2
    max_exact = half // 2
    n = np.abs(rel)
    nl = np.maximum(n, max_exact).astype(np.float32)
    large = max_exact + (np.log(nl / max_exact) / np.log(REL_MAX_DISTANCE / max_exact)
                         * (half - max_exact)).astype(np.int32)
    large = np.minimum(large, half - 1)
    b = np.where(n < max_exact, n, large) + (rel > 0).astype(np.int32) * half
    return b.astype(np.int32)


def _bias_kernel(band_ref, o_ref):
    x = jnp.broadcast_to(band_ref[0, 0] * LOG2E, (ATTN_QB, BIAS_PAD))
    for variant, shift in enumerate(BIAS_SHIFTS):
        t = pltpu.roll(x, shift, 1, stride=1, stride_axis=0)
        o_ref[0, variant, 0] = t[:, :ATTN_KB]


def _bias_tiles(rel_bias):
    bands = []
    for dil in DILATIONS:
        offs = dil * np.arange(-BAND, BAND + 1, dtype=np.int32)
        bands.append(rel_bias[jnp.asarray(_t5_bucket(offs))].T.astype(F32))
    band = jnp.pad(jnp.stack(bands), ((0, 0), (0, 0), (0, BIAS_PAD - 2 * BAND - 1)), constant_values=NEG_INF)
    return pl.pallas_call(
        _bias_kernel,
        grid=(len(DILATIONS), ATTN_HEADS),
        in_specs=[pl.BlockSpec((1, 1, 1, BIAS_PAD), lambda p, h: (p, h, 0, 0))],
        out_specs=pl.BlockSpec((1, 3, 1, ATTN_QB, ATTN_KB), lambda p, h: (p, 0, h, 0, 0)),
        out_shape=jax.ShapeDtypeStruct((len(DILATIONS), 3, ATTN_HEADS, ATTN_QB, ATTN_KB), F32),
        compiler_params=_cparams(2, 32),
        name="bias_tiles",
    )(band[:, :, None, :])


def _attn_kernel(q_ref, k_ref, v_ref, bias_ref, o_ref, m_ref, l_ref, qd_ref, kd_ref, vd_ref,
                 qr_ref, kr_ref, vr_ref, ar_ref, mr_ref, lr_ref, *, seq, unroll):
    head0 = lax.broadcasted_iota(jnp.int32, (ATTN_QB, LANES), 1) < HEAD_DIM
    zero = jnp.zeros((ATTN_QB, LANES), BF16)
    nsteps = seq // ATTN_QB
    last = len(DILATIONS) - 1

    def block_rows(t):
        return pl.ds(pl.multiple_of(t * ATTN_QB, ATTN_QB), ATTN_QB)

    def rows_in(t, dil, base):
        nblk = seq // dil // ATTN_QB
        r, i = t // nblk, t % nblk
        if dil == base:
            return block_rows(t)
        start = (r % base) * (seq // base) + r // base + (dil // base) * ATTN_QB * i
        return pl.ds(start, ATTN_QB, stride=dil // base)

    for pi, dil in enumerate(DILATIONS):
        sub_len = seq // dil
        nblk = sub_len // ATTN_QB
        prev = DILATIONS[pi - 1] if pi else 1
        relay_in = pi >= 2
        relay_out = 0 < pi < last

        def gather(t, carry, dil=dil, prev=prev, relay_in=relay_in, relay_out=relay_out):
            dst = block_rows(t)
            if relay_in:
                src = rows_in(t, dil, prev)
                q, k, v = qr_ref[src, :], kr_ref[src, :], vr_ref[src, :]
            else:
                src = rows_in(t, dil, 1)
                q, k, v = q_ref[0, src, :], k_ref[0, src, :], v_ref[0, src, :]
            if relay_out:
                qr_ref[dst, :], kr_ref[dst, :], vr_ref[dst, :] = q, k, v
            qd_ref[dst, :] = (q * (HEAD_DIM ** -0.5 * LOG2E)).astype(BF16)
            kd_ref[dst, :] = k.astype(BF16)
            vd_ref[dst, :] = v.astype(BF16)
            return carry

        lax.fori_loop(0, nsteps, gather, 0, unroll=unroll)

        def step(t, carry, pi=pi, dil=dil, prev=prev, sub_len=sub_len, nblk=nblk,
                 relay_in=relay_in, relay_out=relay_out):
            i = t % nblk
            kstart = jnp.clip(i * ATTN_QB - BAND, 0, sub_len - ATTN_KB)
            variant = jnp.where(i == 0, 0, jnp.where(i == nblk - 1, 2, 1))
            krows = pl.ds(pl.multiple_of((t - i) * ATTN_QB + kstart, BAND), ATTN_KB)
            q = qd_ref[block_rows(t), :]
            k = kd_ref[krows, :]
            v = vd_ref[krows, :]
            pv, den, top = [], [], []
            for h in range(2):
                qh = jnp.where(head0, q, zero) if h == 0 else jnp.where(head0, zero, q)
                s = lax.dot_general(qh, k, (((1,), (1,)), ((), ())), preferred_element_type=F32)
                s = s + bias_ref[pi, variant, h]
                m = jnp.max(s, axis=-1, keepdims=True)
                p = jnp.exp2(s - m)
                pv.append(jnp.dot(p.astype(BF16), v, preferred_element_type=F32))
                den.append(jnp.sum(p, axis=-1, keepdims=True))
                top.append(m)
            acc = jnp.where(head0, pv[0], pv[1])
            l = jnp.where(head0, den[0], den[1])
            m = jnp.where(head0, top[0], top[1])
            if pi > 0:
                if relay_in:
                    src = rows_in(t, dil, prev)
                    acc_run, m_run, l_run = ar_ref[src, :], mr_ref[src, :], lr_ref[src, :]
                else:
                    src = rows_in(t, dil, 1)
                    acc_run, m_run, l_run = o_ref[0, src, :], m_ref[src, :], l_ref[src, :]
                m_new = jnp.maximum(m, m_run)
                w_cur = jnp.exp2(m - m_new)
                w_run = jnp.exp2(m_run - m_new)
                acc = w_cur * acc + w_run * acc_run
                l = w_cur * l + w_run * l_run
                m = m_new
            if pi == last:
                o_ref[0, rows_in(t, dil, 1), :] = acc / l
            elif relay_out:
                dst = block_rows(t)
                ar_ref[dst, :], mr_ref[dst, :], lr_ref[dst, :] = acc, m, l
            else:
                dst = rows_in(t, dil, 1)
                o_ref[0, dst, :], m_ref[dst, :], l_ref[dst, :] = acc, m, l
            return carry

        lax.fori_loop(0, nsteps, step, 0, unroll=unroll)


def _attn(z3, bias, unroll=8):
    bsz, seq, _ = z3.shape
    assert all(seq // d >= 2 * ATTN_QB and seq % (d * ATTN_QB) == 0 for d in DILATIONS)
    assert DILATIONS[0] == 1 and len(DILATIONS) == 3 and DILATIONS[2] % DILATIONS[1] == 0
    col = lambda off: (lambda g, b: (b, 0, off // LANES + g))
    return pl.pallas_call(
        functools.partial(_attn_kernel, seq=seq, unroll=unroll),
        grid=(ATTN_WIDTH // LANES, bsz),
        in_specs=[pl.BlockSpec((1, seq, LANES), col(Q_OFF)),
                  pl.BlockSpec((1, seq, LANES), col(K_OFF)),
                  pl.BlockSpec((1, seq, LANES), col(V_OFF)),
                  pl.BlockSpec((len(DILATIONS), 3, LANES // HEAD_DIM, ATTN_QB, ATTN_KB),
                               lambda g, b: (0, 0, g, 0, 0))],
        out_specs=pl.BlockSpec((1, seq, LANES), lambda g, b: (b, 0, g)),
        out_shape=jax.ShapeDtypeStruct((bsz, seq, ATTN_WIDTH), F32),
        scratch_shapes=([pltpu.VMEM((seq, LANES), F32)] * 2 + [pltpu.VMEM((seq, LANES), BF16)] * 3
                        + [pltpu.VMEM((seq, LANES), F32)] * 6),
        compiler_params=_cparams(2, 56),
        name="attn",
    )(z3, z3, z3, bias)


def _dft_mats():
    n = np.arange(FFT_N)
    ang = 2.0 * np.pi * np.outer(n, n) / FFT_N
    c, s = np.cos(ang), np.sin(ang)
    fa = np.concatenate([c, -s], axis=0)
    fb = np.concatenate([s, c], axis=0)
    ang_t = 2.0 * np.pi * np.outer(n, n) / (FFT_N * FFT_N)
    tw_c = np.broadcast_to(np.cos(ang_t)[:, :, None], (FFT_N, FFT_N, LANES))
    tw_s = np.broadcast_to(np.sin(ang_t)[:, :, None], (FFT_N, FFT_N, LANES))
    eye = np.eye(LANES // HEAD_DIM)
    scale = (FFT_N * FFT_N * HEAD_DIM) ** -0.5
    cc = np.kron(eye, c) * scale
    sc = np.kron(eye, s) * scale
    as_bf = lambda a: jnp.asarray(a, dtype=F32).astype(BF16)
    return (as_bf(fa), as_bf(fb), jnp.asarray(tw_c, F32), jnp.asarray(tw_s, F32), as_bf(cc), as_bf(sc))


def _fnet_kernel(x_ref, fa_ref, fb_ref, twc_ref, tws_ref, cc_ref, sc_ref, w_ref, o_ref,
                 bre_ref, bim_ref, zre_ref, zim_ref, *, epi_rows, unroll):
    def stage_a(s2, carry):
        x = x_ref[0, pl.ds(s2, FFT_N, stride=FFT_N), :].astype(BF16)
        a = jnp.dot(fa_ref[...], x, preferred_element_type=F32)
        ar, ai = a[:FFT_N], a[FFT_N:]
        c, s = twc_ref[s2], tws_ref[s2]
        dst = pl.ds(pl.multiple_of(s2 * FFT_N, FFT_N), FFT_N)
        bre_ref[dst, :] = ar * c + ai * s
        bim_ref[dst, :] = ai * c - ar * s
        return carry

    lax.fori_loop(0, FFT_N, stage_a, 0, unroll=unroll)

    def stage_b(k1, carry):
        src = pl.ds(k1, FFT_N, stride=FFT_N)
        z = (jnp.dot(fa_ref[...], bre_ref[src, :].astype(BF16), preferred_element_type=F32)
             + jnp.dot(fb_ref[...], bim_ref[src, :].astype(BF16), preferred_element_type=F32))
        zre_ref[src, :] = z[:FFT_N]
        zim_ref[src, :] = z[FFT_N:]
        return carry

    lax.fori_loop(0, FFT_N, stage_b, 0, unroll=unroll)

    for e in range(x_ref.shape[1] // epi_rows):
        rows = slice(e * epi_rows, (e + 1) * epi_rows)
        f = (jnp.dot(zre_ref[rows, :].astype(BF16), cc_ref[...], preferred_element_type=F32)
             + jnp.dot(zim_ref[rows, :].astype(BF16), sc_ref[...], preferred_element_type=F32))
        o_ref[0, rows, :] = jnp.dot(f.astype(BF16), w_ref[0], preferred_element_type=F32)


def _fnet(z3, mats, w_pairs):
    bsz, seq, _ = z3.shape
    assert seq == FFT_N * FFT_N
    fa, fb, tw_c, tw_s, cc, sc = mats
    const = lambda shape: pl.BlockSpec(shape, lambda b, j: (0,) * len(shape))
    return pl.pallas_call(
        functools.partial(_fnet_kernel, epi_rows=512, unroll=8),
        grid=(bsz, FNET_WIDTH // LANES),
        in_specs=[pl.BlockSpec((1, seq, LANES), lambda b, j: (b, 0, C_OFF // LANES + j)),
                  const((2 * FFT_N, FFT_N)), const((2 * FFT_N, FFT_N)),
                  const((FFT_N, FFT_N, LANES)), const((FFT_N, FFT_N, LANES)),
                  const((LANES, LANES)), const((LANES, LANES)),
                  pl.BlockSpec((1, LANES, LANES), lambda b, j: (j, 0, 0))],
        out_specs=pl.BlockSpec((1, seq, LANES), lambda b, j: (b, 0, j)),
        out_shape=jax.ShapeDtypeStruct((bsz, seq, FNET_WIDTH), F32),
        scratch_shapes=[pltpu.VMEM((seq, LANES), F32)] * 4,
        compiler_params=_cparams(2, 40),
        name="fnet",
    )(z3, fa, fb, tw_c, tw_s, cc, sc, w_pairs)


def _out_proj_kernel(a_ref, b_ref, c_ref, g_ref, w_ref, x_ref, o_ref, mix_ref):
    j = pl.program_id(1)

    @pl.when(j == 0)
    def _():
        o1, o2 = GMLP_WIDTH, GMLP_WIDTH + ATTN_WIDTH
        mix_ref[:, :o1] = _rms(a_ref[...].astype(F32), g_ref[:, :o1]).astype(BF16)
        mix_ref[:, o1:o2] = _rms(b_ref[...], g_ref[:, o1:o2]).astype(BF16)
        mix_ref[:, o2:] = _rms(c_ref[...], g_ref[:, o2:]).astype(BF16)

    o_ref[...] = x_ref[...] + jnp.dot(mix_ref[...], w_ref[...], preferred_element_type=F32)


def _out_proj(a_out, b_out, c_out, gain, w, layer, x2, tm=1024, tn=512):
    t = x2.shape[0]
    return pl.pallas_call(
        _out_proj_kernel,
        grid=(t // tm, D_MODEL // tn),
        in_specs=[pl.BlockSpec((tm, GMLP_WIDTH), lambda i, j: (i, 0)),
                  pl.BlockSpec((tm, ATTN_WIDTH), lambda i, j: (i, 0)),
                  pl.BlockSpec((tm, FNET_WIDTH), lambda i, j: (i, 0)),
                  pl.BlockSpec((1, D_MODEL), lambda i, j: (0, 0)),
                  pl.BlockSpec((None, D_MODEL, tn), lambda i, j: (layer, 0, j)),
                  pl.BlockSpec((tm, tn), lambda i, j: (i, j))],
        out_specs=pl.BlockSpec((tm, tn), lambda i, j: (i, j)),
        out_shape=jax.ShapeDtypeStruct((t, D_MODEL), F32),
        scratch_shapes=[pltpu.VMEM((tm, D_MODEL), BF16)],
        compiler_params=_cparams(2, 56),
        name="out_proj",
    )(a_out, b_out, c_out, gain, w, x2)


def _ffn_up_kernel(x_ref, xp_ref, xn_ref, g_ref, wg_ref, wu_ref, cwg_ref, cwu_ref, cbg_ref, cbu_ref,
                   o_ref, hn_ref, *, tiles_per_seq):
    i = pl.program_id(0)
    j = pl.program_id(1)
    tm = x_ref.shape[0]
    ext = tm + 2 * HALO

    @pl.when(j == 0)
    def _():
        has_prev = (i % tiles_per_seq != 0).astype(F32)
        has_next = (i % tiles_per_seq != tiles_per_seq - 1).astype(F32)
        hn_ref[:HALO] = (_rms(xp_ref[...], g_ref[...]) * has_prev).astype(BF16)
        hn_ref[HALO:HALO + tm] = _rms(x_ref[...], g_ref[...]).astype(BF16)
        hn_ref[HALO + tm:] = (_rms(xn_ref[...], g_ref[...]) * has_next).astype(BF16)

    def branch(w_ref, cw_ref, cb_ref):
        h = jnp.dot(hn_ref[...], w_ref[...], preferred_element_type=F32)
        up = pltpu.roll(h, 1, axis=0)[HALO:HALO + tm]
        dn = pltpu.roll(h, ext - 1, axis=0)[HALO:HALO + tm]
        return up * cw_ref[0:1] + h[HALO:HALO + tm] * cw_ref[1:2] + dn * cw_ref[2:3] + cb_ref[...]

    gate = branch(wg_ref, cwg_ref, cbg_ref)
    up = branch(wu_ref, cwu_ref, cbu_ref)
    o_ref[...] = (gate * jax.nn.sigmoid(gate) * up).astype(BF16)


def _ffn_up(x2, gain, w, layer, conv_w, conv_b, seq, tm=1024, tn=512):
    t = x2.shape[0]
    nj = D_FF // tn
    hb = tm // HALO
    last = t // HALO - 1
    return pl.pallas_call(
        functools.partial(_ffn_up_kernel, tiles_per_seq=seq // tm),
        grid=(t // tm, nj),
        in_specs=[pl.BlockSpec((tm, D_MODEL), lambda i, j: (i, 0)),
                  pl.BlockSpec((HALO, D_MODEL), lambda i, j: (jnp.maximum(i * hb - 1, 0), 0)),
                  pl.BlockSpec((HALO, D_MODEL), lambda i, j: (jnp.minimum((i + 1) * hb, last), 0)),
                  pl.BlockSpec((1, D_MODEL), lambda i, j: (0, 0)),
                  pl.BlockSpec((None, D_MODEL, tn), lambda i, j: (layer, 0, j)),
                  pl.BlockSpec((None, D_MODEL, tn), lambda i, j: (layer, 0, j + nj)),
                  pl.BlockSpec((3, tn), lambda i, j: (0, j)),
                  pl.BlockSpec((3, tn), lambda i, j: (0, j + nj)),
                  pl.BlockSpec((1, tn), lambda i, j: (0, j)),
                  pl.BlockSpec((1, tn), lambda i, j: (0, j + nj))],
        out_specs=pl.BlockSpec((tm, tn), lambda i, j: (i, j)),
        out_shape=jax.ShapeDtypeStruct((t, D_FF), BF16),
        scratch_shapes=[pltpu.VMEM((tm + 2 * HALO, D_MODEL), BF16)],
        compiler_params=_cparams(2, 56),
        name="ffn_up",
    )(x2, x2, x2, gain, w, w, conv_w, conv_w, conv_b, conv_b)


def _ffn_down_kernel(a_ref, w_ref, x_ref, o_ref):
    o_ref[...] = x_ref[...] + jnp.dot(a_ref[...], w_ref[...], preferred_element_type=F32)


def _ffn_down(act, w, layer, x2, tm=1024, tn=512):
    t = x2.shape[0]
    return pl.pallas_call(
        _ffn_down_kernel,
        grid=(t // tm, D_MODEL // tn),
        in_specs=[pl.BlockSpec((tm, D_FF), lambda i, j: (i, 0)),
                  pl.BlockSpec((None, D_FF, tn), lambda i, j: (layer, 0, j)),
                  pl.BlockSpec((tm, tn), lambda i, j: (i, j))],
        out_specs=pl.BlockSpec((tm, tn), lambda i, j: (i, j)),
        out_shape=jax.ShapeDtypeStruct((t, D_MODEL), F32),
        compiler_params=_cparams(2, 56),
        name="ffn_down",
    )(act, w, x2)


def _final_norm_kernel(x_ref, g_ref, o_ref):
    o_ref[...] = _rms(x_ref[...], g_ref[...])


def _final_norm(x2, gain, tm=512):
    t = x2.shape[0]
    return pl.pallas_call(
        _final_norm_kernel,
        grid=(t // tm,),
        in_specs=[pl.BlockSpec((tm, D_MODEL), lambda i: (i, 0)),
                  pl.BlockSpec((1, D_MODEL), lambda i: (0, 0))],
        out_specs=pl.BlockSpec((tm, D_MODEL), lambda i: (i, 0)),
        out_shape=jax.ShapeDtypeStruct((t, D_MODEL), F32),
        compiler_params=_cparams(1, 32),
        name="final_norm",
    )(x2, gain)


def kernel(x, w_in, gmlp_ws, gmlp_b, fnet_w, mix_gain, w_out, norm_mix, norm_ffn, ffn_up, ffn_conv_w,
           ffn_conv_b, ffn_down, rel_bias, final_norm):
    bsz, seq, _ = x.shape
    depth = w_in.shape[0]
    t = bsz * seq
    x2 = x.reshape(t, D_MODEL)
    mats = _dft_mats()
    bias = _bias_tiles(rel_bias)
    eye = jnp.eye(LANES // HEAD_DIM, dtype=F32)
    w_in, w_out, ffn_up, ffn_down = (w.astype(BF16) for w in (w_in, w_out, ffn_up, ffn_down))
    for l in range(depth):
        z = _proj_in(x2, norm_mix[l][None], w_in, l)
        ws_pairs = gmlp_ws[l].astype(BF16).reshape(GMLP_HEADS // 2, 2 * CHUNK, CHUNK)
        b_tile = jnp.repeat(gmlp_b[l].T, HEAD_DIM, axis=1)
        a_out = _gmlp(z, ws_pairs, b_tile)
        z3 = z.reshape(bsz, seq, IN_WIDTH)
        b_out = _attn(z3, bias).reshape(t, ATTN_WIDTH)
        w_pairs = jnp.einsum("gh,pgce->pgche", eye, fnet_w[l].reshape(-1, 2, HEAD_DIM, HEAD_DIM))
        w_pairs = w_pairs.reshape(-1, LANES, LANES).astype(BF16)
        c_out = _fnet(z3, mats, w_pairs).reshape(t, FNET_WIDTH)
        x2 = _out_proj(a_out, b_out, c_out, mix_gain[l][None], w_out, l, x2)
        act = _ffn_up(x2, norm_ffn[l][None], ffn_up, l, ffn_conv_w[l], ffn_conv_b[l][None], seq)
        x2 = _ffn_down(act, ffn_down, l, x2)
    return _final_norm(x2, final_norm[None]).reshape(bsz, seq, D_MODEL)
```

```python
import functools

import numpy as np
import jax
import jax.numpy as jnp
from jax import lax
from jax.experimental import pallas as pl
from jax.experimental.pallas import tpu as pltpu

D_MODEL = 2048
HEAD_DIM = 64
GMLP_WIDTH = 512
ATTN_WIDTH = 1024
FNET_WIDTH = 512
GMLP_HEADS = GMLP_WIDTH // HEAD_DIM
ATTN_HEADS = ATTN_WIDTH // HEAD_DIM
FNET_GROUPS = FNET_WIDTH // HEAD_DIM
CHUNK = 128
DILATIONS = (1, 4, 16)
REL_BUCKETS = 32
REL_MAX_DISTANCE = 1024
D_FF = 5632
EPS = 1e-6
IN_WIDTH = 2 * GMLP_WIDTH + 3 * ATTN_WIDTH + FNET_WIDTH
NEG_INF = -1e30
LOG2E = 1.4426950408889634

Q_OFF = 2 * GMLP_WIDTH
K_OFF = Q_OFF + ATTN_WIDTH
V_OFF = K_OFF + ATTN_WIDTH
C_OFF = V_OFF + ATTN_WIDTH

LANES = 128
MXU_N = 256
BAND = 64
ATTN_QB = 128
ATTN_KB = ATTN_QB + 2 * BAND
FFT_N = 64
HALO = 16
BIAS_PAD = 512
BIAS_SHIFTS = (BIAS_PAD - BAND, 0, BAND)

F32 = jnp.float32
BF16 = jnp.bfloat16
MIB = 1024 * 1024


def _cparams(n_axes, vmem_mib):
    return pltpu.CompilerParams(dimension_semantics=("arbitrary",) * n_axes,
                                vmem_limit_bytes=vmem_mib * MIB)


def _rms(x, g):
    return x * lax.rsqrt(jnp.mean(x * x, axis=-1, keepdims=True) + EPS) * g


def _proj_in_kernel(x_ref, g_ref, w_ref, o_ref, xn_ref, *, gelu_tiles):
    j = pl.program_id(1)

    @pl.when(j == 0)
    def _():
        xn_ref[...] = _rms(x_ref[...], g_ref[...]).astype(BF16)

    o_ref[...] = jnp.dot(xn_ref[...], w_ref[...], preferred_element_type=F32)

    @pl.when(j < gelu_tiles)
    def _():
        acc = o_ref[...]
        o_ref[...] = 0.5 * acc * (1.0 + lax.erf(acc * np.float32(2.0 ** -0.5)))


def _proj_in(x2, gain, w, layer, tm=1024, tn=512):
    t = x2.shape[0]
    return pl.pallas_call(
        functools.partial(_proj_in_kernel, gelu_tiles=Q_OFF // tn),
        grid=(t // tm, IN_WIDTH // tn),
        in_specs=[pl.BlockSpec((tm, D_MODEL), lambda i, j: (i, 0)),
                  pl.BlockSpec((1, D_MODEL), lambda i, j: (0, 0)),
                  pl.BlockSpec((None, D_MODEL, tn), lambda i, j: (layer, 0, j))],
        out_specs=pl.BlockSpec((tm, tn), lambda i, j: (i, j)),
        out_shape=jax.ShapeDtypeStruct((t, IN_WIDTH), F32),
        scratch_shapes=[pltpu.VMEM((tm, D_MODEL), BF16)],
        compiler_params=_cparams(2, 56),
        name="proj_in",
    )(x2, gain, w)


def _gmlp_kernel(u_ref, v_ref, ws_ref, b_ref, o_ref, *, chunks):
    lane = lax.broadcasted_iota(jnp.int32, (CHUNK, 2 * HEAD_DIM), 1)
    for c in range(chunks):
        rows = slice(c * CHUNK, (c + 1) * CHUNK)
        for p in range(GMLP_HEADS // 2):
            cols = slice(p * 2 * HEAD_DIM, (p + 1) * 2 * HEAD_DIM)
            both = jnp.dot(ws_ref[p], v_ref[rows, cols].astype(BF16), preferred_element_type=F32)
            gate = jnp.where(lane < HEAD_DIM, both[:CHUNK], both[CHUNK:]) + b_ref[:, cols]
            o_ref[rows, cols] = (u_ref[rows, cols] * gate).astype(BF16)


def _gmlp(z, ws_pairs, bias_tile, chunks=4):
    t = z.shape[0]
    tm = chunks * CHUNK
    return pl.pallas_call(
        functools.partial(_gmlp_kernel, chunks=chunks),
        grid=(t // tm,),
        in_specs=[pl.BlockSpec((tm, GMLP_WIDTH), lambda i: (i, 0)),
                  pl.BlockSpec((tm, GMLP_WIDTH), lambda i: (i, 1)),
                  pl.BlockSpec((GMLP_HEADS // 2, 2 * CHUNK, CHUNK), lambda i: (0, 0, 0)),
                  pl.BlockSpec((CHUNK, GMLP_WIDTH), lambda i: (0, 0))],
        out_specs=pl.BlockSpec((tm, GMLP_WIDTH), lambda i: (i, 0)),
        out_shape=jax.ShapeDtypeStruct((t, GMLP_WIDTH), BF16),
        compiler_params=_cparams(1, 32),
        name="gmlp",
    )(z, z, ws_pairs, bias_tile)


def _t5_bucket(rel):
    half = REL_BUCKETS // 2
    max_exact = half // 2
    n = np.abs(rel)
    nl = np.maximum(n, max_exact).astype(np.float32)
    large = max_exact + (np.log(nl / max_exact) / np.log(REL_MAX_DISTANCE / max_exact)
                         * (half - max_exact)).astype(np.int32)
    large = np.minimum(large, half - 1)
    b = np.where(n < max_exact, n, large) + (rel > 0).astype(np.int32) * half
    return b.astype(np.int32)


def _bias_kernel(band_ref, o_ref):
    x = jnp.broadcast_to(band_ref[0, 0] * LOG2E, (ATTN_QB, BIAS_PAD))
    for variant, shift in enumerate(BIAS_SHIFTS):
        t = pltpu.roll(x, shift, 1, stride=1, stride_axis=0)
        o_ref[0, variant, 0] = t[:, :ATTN_KB]


def _bias_tiles(rel_bias):
    bands = []
    for dil in DILATIONS:
        offs = dil * np.arange(-BAND, BAND + 1, dtype=np.int32)
        bands.append(rel_bias[jnp.asarray(_t5_bucket(offs))].T.astype(F32))
    band = jnp.pad(jnp.stack(bands), ((0, 0), (0, 0), (0, BIAS_PAD - 2 * BAND - 1)), constant_values=NEG_INF)
    return pl.pallas_call(
        _bias_kernel,
        grid=(len(DILATIONS), ATTN_HEADS),
        in_specs=[pl.BlockSpec((1, 1, 1, BIAS_PAD), lambda p, h: (p, h, 0, 0))],
        out_specs=pl.BlockSpec((1, 3, 1, ATTN_QB, ATTN_KB), lambda p, h: (p, 0, h, 0, 0)),
        out_shape=jax.ShapeDtypeStruct((len(DILATIONS), 3, ATTN_HEADS, ATTN_QB, ATTN_KB), F32),
        compiler_params=_cparams(2, 32),
        name="bias_tiles",
    )(band[:, :, None, :])


def _attn_kernel(q_ref, k_ref, v_ref, bias_ref, o_ref, m_ref, l_ref, qd_ref, kd_ref, vd_ref,
                 qr_ref, kr_ref, vr_ref, ar_ref, mr_ref, lr_ref, *, seq, unroll):
    head0 = lax.broadcasted_iota(jnp.int32, (ATTN_QB, LANES), 1) < HEAD_DIM
    zero = jnp.zeros((ATTN_QB, LANES), BF16)
    nsteps = seq // ATTN_QB
    last = len(DILATIONS) - 1

    def block_rows(t):
        return pl.ds(pl.multiple_of(t * ATTN_QB, ATTN_QB), ATTN_QB)

    def rows_in(t, dil, base):
        nblk = seq // dil // ATTN_QB
        r, i = t // nblk, t % nblk
        if dil == base:
            return block_rows(t)
        start = (r % base) * (seq // base) + r // base + (dil // base) * ATTN_QB * i
        return pl.ds(start, ATTN_QB, stride=dil // base)

    for pi, dil in enumerate(DILATIONS):
        sub_len = seq // dil
        nblk = sub_len // ATTN_QB
        prev = DILATIONS[pi - 1] if pi else 1
        relay_in = pi >= 2
        relay_out = 0 < pi < last

        def gather(t, carry, dil=dil, prev=prev, relay_in=relay_in, relay_out=relay_out):
            dst = block_rows(t)
            if relay_in:
                src = rows_in(t, dil, prev)
                q, k, v = qr_ref[src, :], kr_ref[src, :], vr_ref[src, :]
            else:
                src = rows_in(t, dil, 1)
                q, k, v = q_ref[0, src, :], k_ref[0, src, :], v_ref[0, src, :]
            if relay_out:
                qr_ref[dst, :], kr_ref[dst, :], vr_ref[dst, :] = q, k, v
            qd_ref[dst, :] = (q * (HEAD_DIM ** -0.5 * LOG2E)).astype(BF16)
            kd_ref[dst, :] = k.astype(BF16)
            vd_ref[dst, :] = v.astype(BF16)
            return carry

        lax.fori_loop(0, nsteps, gather, 0, unroll=unroll)

        def step(t, carry, pi=pi, dil=dil, prev=prev, sub_len=sub_len, nblk=nblk,
                 relay_in=relay_in, relay_out=relay_out):
            i = t % nblk
            kstart = jnp.clip(i * ATTN_QB - BAND, 0, sub_len - ATTN_KB)
            variant = jnp.where(i == 0, 0, jnp.where(i == nblk - 1, 2, 1))
            krows = pl.ds(pl.multiple_of((t - i) * ATTN_QB + kstart, BAND), ATTN_KB)
            q = qd_ref[block_rows(t), :]
            k = kd_ref[krows, :]
            v = vd_ref[krows, :]
            pv, den, top = [], [], []
            for h in range(2):
                qh = jnp.where(head0, q, zero) if h == 0 else jnp.where(head0, zero, q)
                s = lax.dot_general(qh, k, (((1,), (1,)), ((), ())), preferred_element_type=F32)
                s = s + bias_ref[pi, variant, h]
                m = jnp.max(s, axis=-1, keepdims=True)
                p = jnp.exp2(s - m)
                pv.append(jnp.dot(p.astype(BF16), v, preferred_element_type=F32))
                den.append(jnp.sum(p, axis=-1, keepdims=True))
                top.append(m)
            acc = jnp.where(head0, pv[0], pv[1])
            l = jnp.where(head0, den[0], den[1])
            m = jnp.where(head0, top[0], top[1])
            if pi > 0:
                if relay_in:
                    src = rows_in(t, dil, prev)
                    acc_run, m_run, l_run = ar_ref[src, :], mr_ref[src, :], lr_ref[src, :]
                else:
                    src = rows_in(t, dil, 1)
                    acc_run, m_run, l_run = o_ref[0, src, :], m_ref[src, :], l_ref[src, :]
                m_new = jnp.maximum(m, m_run)
                w_cur = jnp.exp2(m - m_new)
                w_run = jnp.exp2(m_run - m_new)
                acc = w_cur * acc + w_run * acc_run
                l = w_cur * l + w_run * l_run
                m = m_new
            if pi == last:
                o_ref[0, rows_in(t, dil, 1), :] = acc / l
            elif relay_out:
                dst = block_rows(t)
                ar_ref[dst, :], mr_ref[dst, :], lr_ref[dst, :] = acc, m, l
            else:
                dst = rows_in(t, dil, 1)
                o_ref[0, dst, :], m_ref[dst, :], l_ref[dst, :] = acc, m, l
            return carry

        lax.fori_loop(0, nsteps, step, 0, unroll=unroll)


def _attn(z3, bias, unroll=8):
    bsz, seq, _ = z3.shape
    assert all(seq // d >= 2 * ATTN_QB and seq % (d * ATTN_QB) == 0 for d in DILATIONS)
    assert DILATIONS[0] == 1 and len(DILATIONS) == 3 and DILATIONS[2] % DILATIONS[1] == 0
    col = lambda off: (lambda g, b: (b, 0, off // LANES + g))
    return pl.pallas_call(
        functools.partial(_attn_kernel, seq=seq, unroll=unroll),
        grid=(ATTN_WIDTH // LANES, bsz),
        in_specs=[pl.BlockSpec((1, seq, LANES), col(Q_OFF)),
                  pl.BlockSpec((1, seq, LANES), col(K_OFF)),
                  pl.BlockSpec((1, seq, LANES), col(V_OFF)),
                  pl.BlockSpec((len(DILATIONS), 3, LANES // HEAD_DIM, ATTN_QB, ATTN_KB),
                               lambda g, b: (0, 0, g, 0, 0))],
        out_specs=pl.BlockSpec((1, seq, LANES), lambda g, b: (b, 0, g)),
        out_shape=jax.ShapeDtypeStruct((bsz, seq, ATTN_WIDTH), F32),
        scratch_shapes=([pltpu.VMEM((seq, LANES), F32)] * 2 + [pltpu.VMEM((seq, LANES), BF16)] * 3
                        + [pltpu.VMEM((seq, LANES), F32)] * 6),
        compiler_params=_cparams(2, 56),
        name="attn",
    )(z3, z3, z3, bias)


def _dft_mats():
    n = np.arange(FFT_N)
    ang = 2.0 * np.pi * np.outer(n, n) / FFT_N
    c, s = np.cos(ang), np.sin(ang)
    fa = np.concatenate([c, -s], axis=0)
    fb = np.concatenate([s, c], axis=0)
    ang_t = 2.0 * np.pi * np.outer(n, n) / (FFT_N * FFT_N)
    tw_c = np.broadcast_to(np.cos(ang_t)[:, :, None], (FFT_N, FFT_N, LANES))
    tw_s = np.broadcast_to(np.sin(ang_t)[:, :, None], (FFT_N, FFT_N, LANES))
    eye = np.eye(LANES // HEAD_DIM)
    scale = (FFT_N * FFT_N * HEAD_DIM) ** -0.5
    cc = np.kron(eye, c) * scale
    sc = np.kron(eye, s) * scale
    as_bf = lambda a: jnp.asarray(a, dtype=F32).astype(BF16)
    return (as_bf(fa), as_bf(fb), jnp.asarray(tw_c, F32), jnp.asarray(tw_s, F32), as_bf(cc), as_bf(sc))


def _fnet_kernel(x_ref, fa_ref, fb_ref, twc_ref, tws_ref, cc_ref, sc_ref, w_ref, o_ref,
                 bre_ref, bim_ref, zre_ref, zim_ref, *, epi_rows, unroll):
    def stage_a(s2, carry):
        x = x_ref[0, pl.ds(s2, FFT_N, stride=FFT_N), :].astype(BF16)
        a = jnp.dot(fa_ref[...], x, preferred_element_type=F32)
        ar, ai = a[:FFT_N], a[FFT_N:]
        c, s = twc_ref[s2], tws_ref[s2]
        dst = pl.ds(pl.multiple_of(s2 * FFT_N, FFT_N), FFT_N)
        bre_ref[dst, :] = ar * c + ai * s
        bim_ref[dst, :] = ai * c - ar * s
        return carry

    lax.fori_loop(0, FFT_N, stage_a, 0, unroll=unroll)

    def stage_b(k1, carry):
        src = pl.ds(k1, FFT_N, stride=FFT_N)
        z = (jnp.dot(fa_ref[...], bre_ref[src, :].astype(BF16), preferred_element_type=F32)
             + jnp.dot(fb_ref[...], bim_ref[src, :].astype(BF16), preferred_element_type=F32))
        zre_ref[src, :] = z[:FFT_N]
        zim_ref[src, :] = z[FFT_N:]
        return carry

    lax.fori_loop(0, FFT_N, stage_b, 0, unroll=unroll)

    for e in range(x_ref.shape[1] // epi_rows):
        rows = slice(e * epi_rows, (e + 1) * epi_rows)
        f = (jnp.dot(zre_ref[rows, :].astype(BF16), cc_ref[...], preferred_element_type=F32)
             + jnp.dot(zim_ref[rows, :].astype(BF16), sc_ref[...], preferred_element_type=F32))
        o_ref[0, rows, :] = jnp.dot(f.astype(BF16), w_ref[0], preferred_element_type=F32)


def _fnet(z3, mats, w_pairs):
    bsz, seq, _ = z3.shape
    assert seq == FFT_N * FFT_N
    fa, fb, tw_c, tw_s, cc, sc = mats
    const = lambda shape: pl.BlockSpec(shape, lambda b, j: (0,) * len(shape))
    return pl.pallas_call(
        functools.partial(_fnet_kernel, epi_rows=512, unroll=8),
        grid=(bsz, FNET_WIDTH // LANES),
        in_specs=[pl.BlockSpec((1, seq, LANES), lambda b, j: (b, 0, C_OFF // LANES + j)),
                  const((2 * FFT_N, FFT_N)), const((2 * FFT_N, FFT_N)),
                  const((FFT_N, FFT_N, LANES)), const((FFT_N, FFT_N, LANES)),
                  const((LANES, LANES)), const((LANES, LANES)),
                  pl.BlockSpec((1, LANES, LANES), lambda b, j: (j, 0, 0))],
        out_specs=pl.BlockSpec((1, seq, LANES), lambda b, j: (b, 0, j)),
        out_shape=jax.ShapeDtypeStruct((bsz, seq, FNET_WIDTH), F32),
        scratch_shapes=[pltpu.VMEM((seq, LANES), F32)] * 4,
        compiler_params=_cparams(2, 40),
        name="fnet",
    )(z3, fa, fb, tw_c, tw_s, cc, sc, w_pairs)


def _out_proj_kernel(a_ref, b_ref, c_ref, g_ref, w_ref, x_ref, o_ref, mix_ref):
    j = pl.program_id(1)

    @pl.when(j == 0)
    def _():
        o1, o2 = GMLP_WIDTH, GMLP_WIDTH + ATTN_WIDTH
        mix_ref[:, :o1] = _rms(a_ref[...].astype(F32), g_ref[:, :o1]).astype(BF16)
        mix_ref[:, o1:o2] = _rms(b_ref[...], g_ref[:, o1:o2]).astype(BF16)
        mix_ref[:, o2:] = _rms(c_ref[...], g_ref[:, o2:]).astype(BF16)

    o_ref[...] = x_ref[...] + jnp.dot(mix_ref[...], w_ref[...], preferred_element_type=F32)


def _out_proj(a_out, b_out, c_out, gain, w, layer, x2, tm=512, tn=D_MODEL):
    t = x2.shape[0]
    return pl.pallas_call(
        _out_proj_kernel,
        grid=(t // tm, D_MODEL // tn),
        in_specs=[pl.BlockSpec((tm, GMLP_WIDTH), lambda i, j: (i, 0)),
                  pl.BlockSpec((tm, ATTN_WIDTH), lambda i, j: (i, 0)),
                  pl.BlockSpec((tm, FNET_WIDTH), lambda i, j: (i, 0)),
                  pl.BlockSpec((1, D_MODEL), lambda i, j: (0, 0)),
                  pl.BlockSpec((None, D_MODEL, tn), lambda i, j: (layer, 0, j)),
                  pl.BlockSpec((tm, tn), lambda i, j: (i, j))],
        out_specs=pl.BlockSpec((tm, tn), lambda i, j: (i, j)),
        out_shape=jax.ShapeDtypeStruct((t, D_MODEL), F32),
        scratch_shapes=[pltpu.VMEM((tm, D_MODEL), BF16)],
        compiler_params=_cparams(2, 56),
        name="out_proj",
    )(a_out, b_out, c_out, gain, w, x2)


def _ffn_up_kernel(x_ref, xp_ref, xn_ref, g_ref, wg_ref, wu_ref, cwg_ref, cwu_ref, cbg_ref, cbu_ref,
                   o_ref, hn_ref, *, tiles_per_seq):
    i = pl.program_id(0)
    j = pl.program_id(1)
    tm = x_ref.shape[0]
    ext = tm + 2 * HALO

    @pl.when(j == 0)
    def _():
        has_prev = (i % tiles_per_seq != 0).astype(F32)
        has_next = (i % tiles_per_seq != tiles_per_seq - 1).astype(F32)
        hn_ref[:HALO] = (_rms(xp_ref[...], g_ref[...]) * has_prev).astype(BF16)
        hn_ref[HALO:HALO + tm] = _rms(x_ref[...], g_ref[...]).astype(BF16)
        hn_ref[HALO + tm:] = (_rms(xn_ref[...], g_ref[...]) * has_next).astype(BF16)

    def branch(w_ref, cw_ref, cb_ref):
        h = jnp.dot(hn_ref[...], w_ref[...], preferred_element_type=F32)
        up = pltpu.roll(h, 1, axis=0)[HALO:HALO + tm]
        dn = pltpu.roll(h, ext - 1, axis=0)[HALO:HALO + tm]
        return up * cw_ref[0:1] + h[HALO:HALO + tm] * cw_ref[1:2] + dn * cw_ref[2:3] + cb_ref[...]

    gate = branch(wg_ref, cwg_ref, cbg_ref)
    up = branch(wu_ref, cwu_ref, cbu_ref)
    o_ref[...] = (gate * jax.nn.sigmoid(gate) * up).astype(BF16)


def _ffn_up(x2, gain, w, layer, conv_w, conv_b, seq, tm=1024, tn=512):
    t = x2.shape[0]
    nj = D_FF // tn
    hb = tm // HALO
    last = t // HALO - 1
    return pl.pallas_call(
        functools.partial(_ffn_up_kernel, tiles_per_seq=seq // tm),
        grid=(t // tm, nj),
        in_specs=[pl.BlockSpec((tm, D_MODEL), lambda i, j: (i, 0)),
                  pl.BlockSpec((HALO, D_MODEL), lambda i, j: (jnp.maximum(i * hb - 1, 0), 0)),
                  pl.BlockSpec((HALO, D_MODEL), lambda i, j: (jnp.minimum((i + 1) * hb, last), 0)),
                  pl.BlockSpec((1, D_MODEL), lambda i, j: (0, 0)),
                  pl.BlockSpec((None, D_MODEL, tn), lambda i, j: (layer, 0, j)),
                  pl.BlockSpec((None, D_MODEL, tn), lambda i, j: (layer, 0, j + nj)),
                  pl.BlockSpec((3, tn), lambda i, j: (0, j)),
                  pl.BlockSpec((3, tn), lambda i, j: (0, j + nj)),
                  pl.BlockSpec((1, tn), lambda i, j: (0, j)),
                  pl.BlockSpec((1, tn), lambda i, j: (0, j + nj))],
        out_specs=pl.BlockSpec((tm, tn), lambda i, j: (i, j)),
        out_shape=jax.ShapeDtypeStruct((t, D_FF), BF16),
        scratch_shapes=[pltpu.VMEM((tm + 2 * HALO, D_MODEL), BF16)],
        compiler_params=_cparams(2, 56),
        name="ffn_up",
    )(x2, x2, x2, gain, w, w, conv_w, conv_w, conv_b, conv_b)


def _ffn_down_kernel(a_ref, w_ref, x_ref, o_ref):
    o_ref[...] = x_ref[...] + jnp.dot(a_ref[...], w_ref[...], preferred_element_type=F32)


def _ffn_down(act, w, layer, x2, tm=1024, tn=512):
    t = x2.shape[0]
    return pl.pallas_call(
        _ffn_down_kernel,
        grid=(t // tm, D_MODEL // tn),
        in_specs=[pl.BlockSpec((tm, D_FF), lambda i, j: (i, 0)),
                  pl.BlockSpec((None, D_FF, tn), lambda i, j: (layer, 0, j)),
                  pl.BlockSpec((tm, tn), lambda i, j: (i, j))],
        out_specs=pl.BlockSpec((tm, tn), lambda i, j: (i, j)),
        out_shape=jax.ShapeDtypeStruct((t, D_MODEL), F32),
        compiler_params=_cparams(2, 56),
        name="ffn_down",
    )(act, w, x2)


def _final_norm_kernel(x_ref, g_ref, o_ref):
    o_ref[...] = _rms(x_ref[...], g_ref[...])


def _final_norm(x2, gain, tm=512):
    t = x2.shape[0]
    return pl.pallas_call(
        _final_norm_kernel,
        grid=(t // tm,),
        in_specs=[pl.BlockSpec((tm, D_MODEL), lambda i: (i, 0)),
                  pl.BlockSpec((1, D_MODEL), lambda i: (0, 0))],
        out_specs=pl.BlockSpec((tm, D_MODEL), lambda i: (i, 0)),
        out_shape=jax.ShapeDtypeStruct((t, D_MODEL), F32),
        compiler_params=_cparams(1, 32),
        name="final_norm",
    )(x2, gain)


def kernel(x, w_in, gmlp_ws, gmlp_b, fnet_w, mix_gain, w_out, norm_mix, norm_ffn, ffn_up, ffn_conv_w,
           ffn_conv_b, ffn_down, rel_bias, final_norm):
    bsz, seq, _ = x.shape
    depth = w_in.shape[0]
    t = bsz * seq
    x2 = x.reshape(t, D_MODEL)
    mats = _dft_mats()
    bias = _bias_tiles(rel_bias)
    eye = jnp.eye(LANES // HEAD_DIM, dtype=F32)
    w_in, w_out, ffn_up, ffn_down = (w.astype(BF16) for w in (w_in, w_out, ffn_up, ffn_down))
    for l in range(depth):
        z = _proj_in(x2, norm_mix[l][None], w_in, l)
        ws_pairs = gmlp_ws[l].astype(BF16).reshape(GMLP_HEADS // 2, 2 * CHUNK, CHUNK)
        b_tile = jnp.repeat(gmlp_b[l].T, HEAD_DIM, axis=1)
        a_out = _gmlp(z, ws_pairs, b_tile)
        z3 = z.reshape(bsz, seq, IN_WIDTH)
        b_out = _attn(z3, bias).reshape(t, ATTN_WIDTH)
        w_pairs = jnp.einsum("gh,pgce->pgche", eye, fnet_w[l].reshape(-1, 2, HEAD_DIM, HEAD_DIM))
        w_pairs = w_pairs.reshape(-1, LANES, LANES).astype(BF16)
        c_out = _fnet(z3, mats, w_pairs).reshape(t, FNET_WIDTH)
        x2 = _out_proj(a_out, b_out, c_out, mix_gain[l][None], w_out, l, x2)
        act = _ffn_up(x2, norm_ffn[l][None], ffn_up, l, ffn_conv_w[l], ffn_conv_b[l][None], seq)
        x2 = _ffn_down(act, ffn_down, l, x2)
    return _final_norm(x2, final_norm[None]).reshape(bsz, seq, D_MODEL)
```

```python
import functools

import numpy as np
import jax
import jax.numpy as jnp
from jax import lax
from jax.experimental import pallas as pl
from jax.experimental.pallas import tpu as pltpu

D_MODEL = 2048
HEAD_DIM = 64
GMLP_WIDTH = 512
ATTN_WIDTH = 1024
FNET_WIDTH = 512
GMLP_HEADS = GMLP_WIDTH // HEAD_DIM
ATTN_HEADS = ATTN_WIDTH // HEAD_DIM
FNET_GROUPS = FNET_WIDTH // HEAD_DIM
CHUNK = 128
DILATIONS = (1, 4, 16)
REL_BUCKETS = 32
REL_MAX_DISTANCE = 1024
D_FF = 5632
EPS = 1e-6
IN_WIDTH = 2 * GMLP_WIDTH + 3 * ATTN_WIDTH + FNET_WIDTH
NEG_INF = -1e30
LOG2E = 1.4426950408889634

Q_OFF = 2 * GMLP_WIDTH
K_OFF = Q_OFF + ATTN_WIDTH
V_OFF = K_OFF + ATTN_WIDTH
C_OFF = V_OFF + ATTN_WIDTH

LANES = 128
MXU_N = 256
BAND = 64
ATTN_QB = 128
ATTN_KB = ATTN_QB + 2 * BAND
FFT_N = 64
HALO = 16
BIAS_PAD = 512
BIAS_SHIFTS = (BIAS_PAD - BAND, 0, BAND)

F32 = jnp.float32
BF16 = jnp.bfloat16
MIB = 1024 * 1024


def _cparams(n_axes, vmem_mib):
    return pltpu.CompilerParams(dimension_semantics=("arbitrary",) * n_axes,
                                vmem_limit_bytes=vmem_mib * MIB)


def _rms(x, g):
    return x * lax.rsqrt(jnp.mean(x * x, axis=-1, keepdims=True) + EPS) * g


def _proj_in_kernel(x_ref, g_ref, w_ref, o_ref, xn_ref, *, gelu_tiles):
    j = pl.program_id(1)

    @pl.when(j == 0)
    def _():
        xn_ref[...] = _rms(x_ref[...], g_ref[...]).astype(BF16)

    o_ref[...] = jnp.dot(xn_ref[...], w_ref[...], preferred_element_type=F32)

    @pl.when(j < gelu_tiles)
    def _():
        acc = o_ref[...]
        o_ref[...] = 0.5 * acc * (1.0 + lax.erf(acc * np.float32(2.0 ** -0.5)))


def _proj_in(x2, gain, w, layer, tm=1024, tn=512):
    t = x2.shape[0]
    return pl.pallas_call(
        functools.partial(_proj_in_kernel, gelu_tiles=Q_OFF // tn),
        grid=(t // tm, IN_WIDTH // tn),
        in_specs=[pl.BlockSpec((tm, D_MODEL), lambda i, j: (i, 0)),
                  pl.BlockSpec((1, D_MODEL), lambda i, j: (0, 0)),
                  pl.BlockSpec((None, D_MODEL, tn), lambda i, j: (layer, 0, j))],
        out_specs=pl.BlockSpec((tm, tn), lambda i, j: (i, j)),
        out_shape=jax.ShapeDtypeStruct((t, IN_WIDTH), F32),
        scratch_shapes=[pltpu.VMEM((tm, D_MODEL), BF16)],
        compiler_params=_cparams(2, 56),
        name="proj_in",
    )(x2, gain, w)


def _gmlp_kernel(u_ref, v_ref, ws_ref, b_ref, o_ref, *, chunks):
    lane = lax.broadcasted_iota(jnp.int32, (CHUNK, 2 * HEAD_DIM), 1)
    for c in range(chunks):
        rows = slice(c * CHUNK, (c + 1) * CHUNK)
        for p in range(GMLP_HEADS // 2):
            cols = slice(p * 2 * HEAD_DIM, (p + 1) * 2 * HEAD_DIM)
            both = jnp.dot(ws_ref[p], v_ref[rows, cols].astype(BF16), preferred_element_type=F32)
            gate = jnp.where(lane < HEAD_DIM, both[:CHUNK], both[CHUNK:]) + b_ref[:, cols]
            o_ref[rows, cols] = (u_ref[rows, cols] * gate).astype(BF16)


def _gmlp(z, ws_pairs, bias_tile, chunks=4):
    t = z.shape[0]
    tm = chunks * CHUNK
    return pl.pallas_call(
        functools.partial(_gmlp_kernel, chunks=chunks),
        grid=(t // tm,),
        in_specs=[pl.BlockSpec((tm, GMLP_WIDTH), lambda i: (i, 0)),
                  pl.BlockSpec((tm, GMLP_WIDTH), lambda i: (i, 1)),
                  pl.BlockSpec((GMLP_HEADS // 2, 2 * CHUNK, CHUNK), lambda i: (0, 0, 0)),
                  pl.BlockSpec((CHUNK, GMLP_WIDTH), lambda i: (0, 0))],
        out_specs=pl.BlockSpec((tm, GMLP_WIDTH), lambda i: (i, 0)),
        out_shape=jax.ShapeDtypeStruct((t, GMLP_WIDTH), BF16),
        compiler_params=_cparams(1, 32),
        name="gmlp",
    )(z, z, ws_pairs, bias_tile)


def _t5_bucket(rel):
    half = REL_BUCKETS // 2
    max_exact = half // 2
    n = np.abs(rel)
    nl = np.maximum(n, max_exact).astype(np.float32)
    large = max_exact + (np.log(nl / max_exact) / np.log(REL_MAX_DISTANCE / max_exact)
                         * (half - max_exact)).astype(np.int32)
    large = np.minimum(large, half - 1)
    b = np.where(n < max_exact, n, large) + (rel > 0).astype(np.int32) * half
    return b.astype(np.int32)


def _bias_kernel(band_ref, o_ref):
    x = jnp.broadcast_to(band_ref[0, 0] * LOG2E, (ATTN_QB, BIAS_PAD))
    for variant, shift in enumerate(BIAS_SHIFTS):
        t = pltpu.roll(x, shift, 1, stride=1, stride_axis=0)
        o_ref[0, variant, 0] = t[:, :ATTN_KB]


def _bias_tiles(rel_bias):
    bands = []
    for dil in DILATIONS:
        offs = dil * np.arange(-BAND, BAND + 1, dtype=np.int32)
        bands.append(rel_bias[jnp.asarray(_t5_bucket(offs))].T.astype(F32))
    band = jnp.pad(jnp.stack(bands), ((0, 0), (0, 0), (0, BIAS_PAD - 2 * BAND - 1)), constant_values=NEG_INF)
    return pl.pallas_call(
        _bias_kernel,
        grid=(len(DILATIONS), ATTN_HEADS),
        in_specs=[pl.BlockSpec((1, 1, 1, BIAS_PAD), lambda p, h: (p, h, 0, 0))],
        out_specs=pl.BlockSpec((1, 3, 1, ATTN_QB, ATTN_KB), lambda p, h: (p, 0, h, 0, 0)),
        out_shape=jax.ShapeDtypeStruct((len(DILATIONS), 3, ATTN_HEADS, ATTN_QB, ATTN_KB), F32),
        compiler_params=_cparams(2, 32),
        name="bias_tiles",
    )(band[:, :, None, :])


def _attn_kernel(q_ref, k_ref, v_ref, bias_ref, o_ref, m_ref, l_ref, qd_ref, kd_ref, vd_ref,
                 qr_ref, kr_ref, vr_ref, ar_ref, mr_ref, lr_ref, *, seq, unroll):
    head0 = lax.broadcasted_iota(jnp.int32, (ATTN_QB, LANES), 1) < HEAD_DIM
    zero = jnp.zeros((ATTN_QB, LANES), BF16)
    nsteps = seq // ATTN_QB
    last = len(DILATIONS) - 1

    def block_rows(t):
        return pl.ds(pl.multiple_of(t * ATTN_QB, ATTN_QB), ATTN_QB)

    def rows_in(t, dil, base):
        nblk = seq // dil // ATTN_QB
        r, i = t // nblk, t % nblk
        if dil == base:
            return block_rows(t)
        start = (r % base) * (seq // base) + r // base + (dil // base) * ATTN_QB * i
        return pl.ds(start, ATTN_QB, stride=dil // base)

    for pi, dil in enumerate(DILATIONS):
        sub_len = seq // dil
        nblk = sub_len // ATTN_QB
        prev = DILATIONS[pi - 1] if pi else 1
        relay_in = pi >= 2
        relay_out = 0 < pi < last

        def gather(t, carry, dil=dil, prev=prev, relay_in=relay_in, relay_out=relay_out):
            dst = block_rows(t)
            if relay_in:
                src = rows_in(t, dil, prev)
                q, k, v = qr_ref[src, :], kr_ref[src, :], vr_ref[src, :]
            else:
                src = rows_in(t, dil, 1)
                q, k, v = q_ref[0, src, :], k_ref[0, src, :], v_ref[0, src, :]
            if relay_out:
                qr_ref[dst, :], kr_ref[dst, :], vr_ref[dst, :] = q, k, v
            qd_ref[dst, :] = (q * (HEAD_DIM ** -0.5 * LOG2E)).astype(BF16)
            kd_ref[dst, :] = k.astype(BF16)
            vd_ref[dst, :] = v.astype(BF16)
            return carry

        lax.fori_loop(0, nsteps, gather, 0, unroll=unroll)

        def step(t, carry, pi=pi, dil=dil, prev=prev, sub_len=sub_len, nblk=nblk,
                 relay_in=relay_in, relay_out=relay_out):
            i = t % nblk
            kstart = jnp.clip(i * ATTN_QB - BAND, 0, sub_len - ATTN_KB)
            variant = jnp.where(i == 0, 0, jnp.where(i == nblk - 1, 2, 1))
            krows = pl.ds(pl.multiple_of((t - i) * ATTN_QB + kstart, BAND), ATTN_KB)
            q = qd_ref[block_rows(t), :]
            k = kd_ref[krows, :]
            v = vd_ref[krows, :]
            pv, den, top = [], [], []
            for h in range(2):
                qh = jnp.where(head0, q, zero) if h == 0 else jnp.where(head0, zero, q)
                s = lax.dot_general(qh, k, (((1,), (1,)), ((), ())), preferred_element_type=F32)
                s = s + bias_ref[pi, variant, h]
                m = jnp.max(s, axis=-1, keepdims=True)
                p = jnp.exp2(s - m)
                pv.append(jnp.dot(p.astype(BF16), v, preferred_element_type=F32))
                den.append(jnp.sum(p, axis=-1, keepdims=True))
                top.append(m)
            acc = jnp.where(head0, pv[0], pv[1])
            l = jnp.where(head0, den[0], den[1])
            m = jnp.where(head0, top[0], top[1])
            if pi > 0:
                if relay_in:
                    src = rows_in(t, dil, prev)
                    acc_run, m_run, l_run = ar_ref[src, :], mr_ref[src, :], lr_ref[src, :]
                else:
                    src = rows_in(t, dil, 1)
                    acc_run, m_run, l_run = o_ref[0, src, :], m_ref[src, :], l_ref[src, :]
                m_new = jnp.maximum(m, m_run)
                w_cur = jnp.exp2(m - m_new)
                w_run = jnp.exp2(m_run - m_new)
                acc = w_cur * acc + w_run * acc_run
                l = w_cur * l + w_run * l_run
                m = m_new
            if pi == last:
                o_ref[0, rows_in(t, dil, 1), :] = acc / l
            elif relay_out:
                dst = block_rows(t)
                ar_ref[dst, :], mr_ref[dst, :], lr_ref[dst, :] = acc, m, l
            else:
                dst = rows_in(t, dil, 1)
                o_ref[0, dst, :], m_ref[dst, :], l_ref[dst, :] = acc, m, l
            return carry

        lax.fori_loop(0, nsteps, step, 0, unroll=unroll)


def _attn(z3, bias, unroll=8):
    bsz, seq, _ = z3.shape
    assert all(seq // d >= 2 * ATTN_QB and seq % (d * ATTN_QB) == 0 for d in DILATIONS)
    assert DILATIONS[0] == 1 and len(DILATIONS) == 3 and DILATIONS[2] % DILATIONS[1] == 0
    col = lambda off: (lambda g, b: (b, 0, off // LANES + g))
    return pl.pallas_call(
        functools.partial(_attn_kernel, seq=seq, unroll=unroll),
        grid=(ATTN_WIDTH // LANES, bsz),
        in_specs=[pl.BlockSpec((1, seq, LANES), col(Q_OFF)),
                  pl.BlockSpec((1, seq, LANES), col(K_OFF)),
                  pl.BlockSpec((1, seq, LANES), col(V_OFF)),
                  pl.BlockSpec((len(DILATIONS), 3, LANES // HEAD_DIM, ATTN_QB, ATTN_KB),
                               lambda g, b: (0, 0, g, 0, 0))],
        out_specs=pl.BlockSpec((1, seq, LANES), lambda g, b: (b, 0, g)),
        out_shape=jax.ShapeDtypeStruct((bsz, seq, ATTN_WIDTH), F32),
        scratch_shapes=([pltpu.VMEM((seq, LANES), F32)] * 2 + [pltpu.VMEM((seq, LANES), BF16)] * 3
                        + [pltpu.VMEM((seq, LANES), F32)] * 6),
        compiler_params=_cparams(2, 56),
        name="attn",
    )(z3, z3, z3, bias)


def _dft_mats():
    n = np.arange(FFT_N)
    ang = 2.0 * np.pi * np.outer(n, n) / FFT_N
    c, s = np.cos(ang), np.sin(ang)
    fa = np.concatenate([c, -s], axis=0)
    fb = np.concatenate([s, c], axis=0)
    ang_t = 2.0 * np.pi * np.outer(n, n) / (FFT_N * FFT_N)
    tw_c = np.broadcast_to(np.cos(ang_t)[:, :, None], (FFT_N, FFT_N, LANES))
    tw_s = np.broadcast_to(np.sin(ang_t)[:, :, None], (FFT_N, FFT_N, LANES))
    eye = np.eye(LANES // HEAD_DIM)
    scale = (FFT_N * FFT_N * HEAD_DIM) ** -0.5
    cc = np.kron(eye, c) * scale
    sc = np.kron(eye, s) * scale
    as_bf = lambda a: jnp.asarray(a, dtype=F32).astype(BF16)
    return (as_bf(fa), as_bf(fb), jnp.asarray(tw_c, F32), jnp.asarray(tw_s, F32), as_bf(cc), as_bf(sc))


def _fnet_kernel(x_ref, fa_ref, fb_ref, twc_ref, tws_ref, cc_ref, sc_ref, w_ref, o_ref,
                 bre_ref, bim_ref, zre_ref, zim_ref, *, epi_rows, unroll):
    def stage_a(s2, carry):
        x = x_ref[0, pl.ds(s2, FFT_N, stride=FFT_N), :].astype(BF16)
        a = jnp.dot(fa_ref[...], x, preferred_element_type=F32)
        ar, ai = a[:FFT_N], a[FFT_N:]
        c, s = twc_ref[s2], tws_ref[s2]
        dst = pl.ds(pl.multiple_of(s2 * FFT_N, FFT_N), FFT_N)
        bre_ref[dst, :] = ar * c + ai * s
        bim_ref[dst, :] = ai * c - ar * s
        return carry

    lax.fori_loop(0, FFT_N, stage_a, 0, unroll=unroll)

    def stage_b(k1, carry):
        src = pl.ds(k1, FFT_N, stride=FFT_N)
        z = (jnp.dot(fa_ref[...], bre_ref[src, :].astype(BF16), preferred_element_type=F32)
             + jnp.dot(fb_ref[...], bim_ref[src, :].astype(BF16), preferred_element_type=F32))
        zre_ref[src, :] = z[:FFT_N]
        zim_ref[src, :] = z[FFT_N:]
        return carry

    lax.fori_loop(0, FFT_N, stage_b, 0, unroll=unroll)

    for e in range(x_ref.shape[1] // epi_rows):
        rows = slice(e * epi_rows, (e + 1) * epi_rows)
        f = (jnp.dot(zre_ref[rows, :].astype(BF16), cc_ref[...], preferred_element_type=F32)
             + jnp.dot(zim_ref[rows, :].astype(BF16), sc_ref[...], preferred_element_type=F32))
        o_ref[0, rows, :] = jnp.dot(f.astype(BF16), w_ref[0], preferred_element_type=F32)


def _fnet(z3, mats, w_pairs):
    bsz, seq, _ = z3.shape
    assert seq == FFT_N * FFT_N
    fa, fb, tw_c, tw_s, cc, sc = mats
    const = lambda shape: pl.BlockSpec(shape, lambda b, j: (0,) * len(shape))
    return pl.pallas_call(
        functools.partial(_fnet_kernel, epi_rows=512, unroll=8),
        grid=(bsz, FNET_WIDTH // LANES),
        in_specs=[pl.BlockSpec((1, seq, LANES), lambda b, j: (b, 0, C_OFF // LANES + j)),
                  const((2 * FFT_N, FFT_N)), const((2 * FFT_N, FFT_N)),
                  const((FFT_N, FFT_N, LANES)), const((FFT_N, FFT_N, LANES)),
                  const((LANES, LANES)), const((LANES, LANES)),
                  pl.BlockSpec((1, LANES, LANES), lambda b, j: (j, 0, 0))],
        out_specs=pl.BlockSpec((1, seq, LANES), lambda b, j: (b, 0, j)),
        out_shape=jax.ShapeDtypeStruct((bsz, seq, FNET_WIDTH), F32),
        scratch_shapes=[pltpu.VMEM((seq, LANES), F32)] * 4,
        compiler_params=_cparams(2, 40),
        name="fnet",
    )(z3, fa, fb, tw_c, tw_s, cc, sc, w_pairs)


def _out_proj_kernel(a_ref, b_ref, c_ref, g_ref, w_ref, x_ref, gf_ref, o_ref, hn_ref, mix_ref):
    o1, o2 = GMLP_WIDTH, GMLP_WIDTH + ATTN_WIDTH
    mix_ref[:, :o1] = _rms(a_ref[...].astype(F32), g_ref[:, :o1]).astype(BF16)
    mix_ref[:, o1:o2] = _rms(b_ref[...], g_ref[:, o1:o2]).astype(BF16)
    mix_ref[:, o2:] = _rms(c_ref[...], g_ref[:, o2:]).astype(BF16)
    x = x_ref[...] + jnp.dot(mix_ref[...], w_ref[...], preferred_element_type=F32)
    o_ref[...] = x
    hn_ref[...] = _rms(x, gf_ref[...]).astype(BF16)


def _out_proj(a_out, b_out, c_out, gain, w, layer, x2, ffn_gain, tm=512):
    t = x2.shape[0]
    rows = lambda width: pl.BlockSpec((tm, width), lambda i: (i, 0))
    gains = pl.BlockSpec((1, D_MODEL), lambda i: (0, 0))
    return pl.pallas_call(
        _out_proj_kernel,
        grid=(t // tm,),
        in_specs=[rows(GMLP_WIDTH), rows(ATTN_WIDTH), rows(FNET_WIDTH), gains,
                  pl.BlockSpec((None, D_MODEL, D_MODEL), lambda i: (layer, 0, 0)),
                  rows(D_MODEL), gains],
        out_specs=[rows(D_MODEL), rows(D_MODEL)],
        out_shape=[jax.ShapeDtypeStruct((t, D_MODEL), F32), jax.ShapeDtypeStruct((t, D_MODEL), BF16)],
        scratch_shapes=[pltpu.VMEM((tm, D_MODEL), BF16)],
        compiler_params=_cparams(1, 56),
        name="out_proj",
    )(a_out, b_out, c_out, gain, w, x2, ffn_gain)


def _ffn_up_kernel(h_ref, hp_ref, hx_ref, wg_ref, wu_ref, cwg_ref, cwu_ref, cbg_ref, cbu_ref,
                   o_ref, hn_ref, *, tiles_per_seq):
    i = pl.program_id(0)
    j = pl.program_id(1)
    tm = h_ref.shape[0]
    ext = tm + 2 * HALO

    @pl.when(j == 0)
    def _():
        has_prev = (i % tiles_per_seq != 0).astype(F32)
        has_next = (i % tiles_per_seq != tiles_per_seq - 1).astype(F32)
        hn_ref[:HALO] = (hp_ref[...] * has_prev).astype(BF16)
        hn_ref[HALO:HALO + tm] = h_ref[...]
        hn_ref[HALO + tm:] = (hx_ref[...] * has_next).astype(BF16)

    def branch(w_ref, cw_ref, cb_ref):
        h = jnp.dot(hn_ref[...], w_ref[...], preferred_element_type=F32)
        up = pltpu.roll(h, 1, axis=0)[HALO:HALO + tm]
        dn = pltpu.roll(h, ext - 1, axis=0)[HALO:HALO + tm]
        return up * cw_ref[0:1] + h[HALO:HALO + tm] * cw_ref[1:2] + dn * cw_ref[2:3] + cb_ref[...]

    gate = branch(wg_ref, cwg_ref, cbg_ref)
    up = branch(wu_ref, cwu_ref, cbu_ref)
    o_ref[...] = (gate * jax.nn.sigmoid(gate) * up).astype(BF16)


def _ffn_up(hn, w, layer, conv_w, conv_b, seq, tm=1024, tn=512):
    t = hn.shape[0]
    nj = D_FF // tn
    hb = tm // HALO
    last = t // HALO - 1
    return pl.pallas_call(
        functools.partial(_ffn_up_kernel, tiles_per_seq=seq // tm),
        grid=(t // tm, nj),
        in_specs=[pl.BlockSpec((tm, D_MODEL), lambda i, j: (i, 0)),
                  pl.BlockSpec((HALO, D_MODEL), lambda i, j: (jnp.maximum(i * hb - 1, 0), 0)),
                  pl.BlockSpec((HALO, D_MODEL), lambda i, j: (jnp.minimum((i + 1) * hb, last), 0)),
                  pl.BlockSpec((None, D_MODEL, tn), lambda i, j: (layer, 0, j)),
                  pl.BlockSpec((None, D_MODEL, tn), lambda i, j: (layer, 0, j + nj)),
                  pl.BlockSpec((3, tn), lambda i, j: (0, j)),
                  pl.BlockSpec((3, tn), lambda i, j: (0, j + nj)),
                  pl.BlockSpec((1, tn), lambda i, j: (0, j)),
                  pl.BlockSpec((1, tn), lambda i, j: (0, j + nj))],
        out_specs=pl.BlockSpec((tm, tn), lambda i, j: (i, j)),
        out_shape=jax.ShapeDtypeStruct((t, D_FF), BF16),
        scratch_shapes=[pltpu.VMEM((tm + 2 * HALO, D_MODEL), BF16)],
        compiler_params=_cparams(2, 56),
        name="ffn_up",
    )(hn, hn, hn, w, w, conv_w, conv_w, conv_b, conv_b)


def _ffn_down_kernel(a_ref, w_ref, x_ref, o_ref):
    o_ref[...] = x_ref[...] + jnp.dot(a_ref[...], w_ref[...], preferred_element_type=F32)


def _ffn_down(act, w, layer, x2, tm=1024, tn=512):
    t = x2.shape[0]
    return pl.pallas_call(
        _ffn_down_kernel,
        grid=(t // tm, D_MODEL // tn),
        in_specs=[pl.BlockSpec((tm, D_FF), lambda i, j: (i, 0)),
                  pl.BlockSpec((None, D_FF, tn), lambda i, j: (layer, 0, j)),
                  pl.BlockSpec((tm, tn), lambda i, j: (i, j))],
        out_specs=pl.BlockSpec((tm, tn), lambda i, j: (i, j)),
        out_shape=jax.ShapeDtypeStruct((t, D_MODEL), F32),
        compiler_params=_cparams(2, 56),
        name="ffn_down",
    )(act, w, x2)


def _final_norm_kernel(x_ref, g_ref, o_ref):
    o_ref[...] = _rms(x_ref[...], g_ref[...])


def _final_norm(x2, gain, tm=512):
    t = x2.shape[0]
    return pl.pallas_call(
        _final_norm_kernel,
        grid=(t // tm,),
        in_specs=[pl.BlockSpec((tm, D_MODEL), lambda i: (i, 0)),
                  pl.BlockSpec((1, D_MODEL), lambda i: (0, 0))],
        out_specs=pl.BlockSpec((tm, D_MODEL), lambda i: (i, 0)),
        out_shape=jax.ShapeDtypeStruct((t, D_MODEL), F32),
        compiler_params=_cparams(1, 32),
        name="final_norm",
    )(x2, gain)


def kernel(x, w_in, gmlp_ws, gmlp_b, fnet_w, mix_gain, w_out, norm_mix, norm_ffn, ffn_up, ffn_conv_w,
           ffn_conv_b, ffn_down, rel_bias, final_norm):
    bsz, seq, _ = x.shape
    depth = w_in.shape[0]
    t = bsz * seq
    x2 = x.reshape(t, D_MODEL)
    mats = _dft_mats()
    bias = _bias_tiles(rel_bias)
    eye = jnp.eye(LANES // HEAD_DIM, dtype=F32)
    w_in, w_out, ffn_up, ffn_down = (w.astype(BF16) for w in (w_in, w_out, ffn_up, ffn_down))
    for l in range(depth):
        z = _proj_in(x2, norm_mix[l][None], w_in, l)
        ws_pairs = gmlp_ws[l].astype(BF16).reshape(GMLP_HEADS // 2, 2 * CHUNK, CHUNK)
        b_tile = jnp.repeat(gmlp_b[l].T, HEAD_DIM, axis=1)
        a_out = _gmlp(z, ws_pairs, b_tile)
        z3 = z.reshape(bsz, seq, IN_WIDTH)
        b_out = _attn(z3, bias).reshape(t, ATTN_WIDTH)
        w_pairs = jnp.einsum("gh,pgce->pgche", eye, fnet_w[l].reshape(-1, 2, HEAD_DIM, HEAD_DIM))
        w_pairs = w_pairs.reshape(-1, LANES, LANES).astype(BF16)
        c_out = _fnet(z3, mats, w_pairs).reshape(t, FNET_WIDTH)
        x2, hn = _out_proj(a_out, b_out, c_out, mix_gain[l][None], w_out, l, x2, norm_ffn[l][None])
        act = _ffn_up(hn, ffn_up, l, ffn_conv_w[l], ffn_conv_b[l][None], seq)
        x2 = _ffn_down(act, ffn_down, l, x2)
    return _final_norm(x2, final_norm[None]).reshape(bsz, seq, D_MODEL)
```

```python
import functools

import numpy as np
import jax
import jax.numpy as jnp
from jax import lax
from jax.experimental import pallas as pl
from jax.experimental.pallas import tpu as pltpu

D_MODEL = 2048
HEAD_DIM = 64
GMLP_WIDTH = 512
ATTN_WIDTH = 1024
FNET_WIDTH = 512
GMLP_HEADS = GMLP_WIDTH // HEAD_DIM
ATTN_HEADS = ATTN_WIDTH // HEAD_DIM
FNET_GROUPS = FNET_WIDTH // HEAD_DIM
CHUNK = 128
DILATIONS = (1, 4, 16)
REL_BUCKETS = 32
REL_MAX_DISTANCE = 1024
D_FF = 5632
EPS = 1e-6
IN_WIDTH = 2 * GMLP_WIDTH + 3 * ATTN_WIDTH + FNET_WIDTH
NEG_INF = -1e30
LOG2E = 1.4426950408889634

Q_OFF = 2 * GMLP_WIDTH
K_OFF = Q_OFF + ATTN_WIDTH
V_OFF = K_OFF + ATTN_WIDTH
C_OFF = V_OFF + ATTN_WIDTH

LANES = 128
MXU_N = 256
BAND = 64
ATTN_QB = 128
ATTN_KB = ATTN_QB + 2 * BAND
FFT_N = 64
HALO = 16
BIAS_PAD = 512
BIAS_SHIFTS = (BIAS_PAD - BAND, 0, BAND)

F32 = jnp.float32
BF16 = jnp.bfloat16
MIB = 1024 * 1024
VMEM_LARGE_MIB = 56
VMEM_SMALL_MIB = 40


def _cparams(n_axes, vmem_mib):
    return pltpu.CompilerParams(dimension_semantics=("arbitrary",) * n_axes,
                                vmem_limit_bytes=vmem_mib * MIB)


def _rms(x, g):
    return x * lax.rsqrt(jnp.mean(x * x, axis=-1, keepdims=True) + EPS) * g


def _proj_in_kernel(x_ref, g_ref, w_ref, o_ref, xn_ref, *, gelu_tiles):
    j = pl.program_id(1)

    @pl.when(j == 0)
    def _():
        xn_ref[...] = _rms(x_ref[...], g_ref[...]).astype(BF16)

    o_ref[...] = jnp.dot(xn_ref[...], w_ref[...], preferred_element_type=F32)

    @pl.when(j < gelu_tiles)
    def _():
        acc = o_ref[...]
        o_ref[...] = 0.5 * acc * (1.0 + lax.erf(acc * np.float32(2.0 ** -0.5)))


def _proj_in(x2, gain, w, layer, tm=1024, tn=512):
    t = x2.shape[0]
    return pl.pallas_call(
        functools.partial(_proj_in_kernel, gelu_tiles=Q_OFF // tn),
        grid=(t // tm, IN_WIDTH // tn),
        in_specs=[pl.BlockSpec((tm, D_MODEL), lambda i, j: (i, 0)),
                  pl.BlockSpec((1, D_MODEL), lambda i, j: (0, 0)),
                  pl.BlockSpec((None, D_MODEL, tn), lambda i, j: (layer, 0, j))],
        out_specs=pl.BlockSpec((tm, tn), lambda i, j: (i, j)),
        out_shape=jax.ShapeDtypeStruct((t, IN_WIDTH), F32),
        scratch_shapes=[pltpu.VMEM((tm, D_MODEL), BF16)],
        compiler_params=_cparams(2, VMEM_LARGE_MIB),
        name="proj_in",
    )(x2, gain, w)


def _gmlp_kernel(u_ref, v_ref, ws_ref, b_ref, o_ref, *, chunks):
    lane = lax.broadcasted_iota(jnp.int32, (CHUNK, 2 * HEAD_DIM), 1)
    for c in range(chunks):
        rows = slice(c * CHUNK, (c + 1) * CHUNK)
        for p in range(GMLP_HEADS // 2):
            cols = slice(p * 2 * HEAD_DIM, (p + 1) * 2 * HEAD_DIM)
            both = jnp.dot(ws_ref[p], v_ref[rows, cols].astype(BF16), preferred_element_type=F32)
            gate = jnp.where(lane < HEAD_DIM, both[:CHUNK], both[CHUNK:]) + b_ref[:, cols]
            o_ref[rows, cols] = (u_ref[rows, cols] * gate).astype(BF16)


def _gmlp(z, ws_pairs, bias_tile, chunks=4):
    t = z.shape[0]
    tm = chunks * CHUNK
    return pl.pallas_call(
        functools.partial(_gmlp_kernel, chunks=chunks),
        grid=(t // tm,),
        in_specs=[pl.BlockSpec((tm, GMLP_WIDTH), lambda i: (i, 0)),
                  pl.BlockSpec((tm, GMLP_WIDTH), lambda i: (i, 1)),
                  pl.BlockSpec((GMLP_HEADS // 2, 2 * CHUNK, CHUNK), lambda i: (0, 0, 0)),
                  pl.BlockSpec((CHUNK, GMLP_WIDTH), lambda i: (0, 0))],
        out_specs=pl.BlockSpec((tm, GMLP_WIDTH), lambda i: (i, 0)),
        out_shape=jax.ShapeDtypeStruct((t, GMLP_WIDTH), BF16),
        compiler_params=_cparams(1, VMEM_SMALL_MIB),
        name="gmlp",
    )(z, z, ws_pairs, bias_tile)


def _t5_bucket(rel):
    half = REL_BUCKETS // 2
    max_exact = half // 2
    n = np.abs(rel)
    nl = np.maximum(n, max_exact).astype(np.float32)
    large = max_exact + (np.log(nl / max_exact) / np.log(REL_MAX_DISTANCE / max_exact)
                         * (half - max_exact)).astype(np.int32)
    large = np.minimum(large, half - 1)
    b = np.where(n < max_exact, n, large) + (rel > 0).astype(np.int32) * half
    return b.astype(np.int32)


def _bias_kernel(band_ref, o_ref):
    x = jnp.broadcast_to(band_ref[0, 0] * LOG2E, (ATTN_QB, BIAS_PAD))
    for variant, shift in enumerate(BIAS_SHIFTS):
        t = pltpu.roll(x, shift, 1, stride=1, stride_axis=0)
        o_ref[0, variant, 0] = t[:, :ATTN_KB]


def _bias_tiles(rel_bias):
    bands = []
    for dil in DILATIONS:
        offs = dil * np.arange(-BAND, BAND + 1, dtype=np.int32)
        bands.append(rel_bias[jnp.asarray(_t5_bucket(offs))].T.astype(F32))
    band = jnp.pad(jnp.stack(bands), ((0, 0), (0, 0), (0, BIAS_PAD - 2 * BAND - 1)), constant_values=NEG_INF)
    return pl.pallas_call(
        _bias_kernel,
        grid=(len(DILATIONS), ATTN_HEADS),
        in_specs=[pl.BlockSpec((1, 1, 1, BIAS_PAD), lambda p, h: (p, h, 0, 0))],
        out_specs=pl.BlockSpec((1, 3, 1, ATTN_QB, ATTN_KB), lambda p, h: (p, 0, h, 0, 0)),
        out_shape=jax.ShapeDtypeStruct((len(DILATIONS), 3, ATTN_HEADS, ATTN_QB, ATTN_KB), F32),
        compiler_params=_cparams(2, VMEM_SMALL_MIB),
        name="bias_tiles",
    )(band[:, :, None, :])


def _attn_kernel(q_ref, k_ref, v_ref, bias_ref, o_ref, m_ref, l_ref, qd_ref, kd_ref, vd_ref,
                 qr_ref, kr_ref, vr_ref, ar_ref, mr_ref, lr_ref, *, seq, unroll):
    head0 = lax.broadcasted_iota(jnp.int32, (ATTN_QB, LANES), 1) < HEAD_DIM
    zero = jnp.zeros((ATTN_QB, LANES), BF16)
    nsteps = seq // ATTN_QB
    last = len(DILATIONS) - 1

    def block_rows(t):
        return pl.ds(pl.multiple_of(t * ATTN_QB, ATTN_QB), ATTN_QB)

    def rows_in(t, dil, base):
        nblk = seq // dil // ATTN_QB
        r, i = t // nblk, t % nblk
        if dil == base:
            return block_rows(t)
        start = (r % base) * (seq // base) + r // base + (dil // base) * ATTN_QB * i
        return pl.ds(start, ATTN_QB, stride=dil // base)

    for pi, dil in enumerate(DILATIONS):
        sub_len = seq // dil
        nblk = sub_len // ATTN_QB
        prev = DILATIONS[pi - 1] if pi else 1
        relay_in = pi >= 2
        relay_out = 0 < pi < last

        def gather(t, carry, dil=dil, prev=prev, relay_in=relay_in, relay_out=relay_out):
            dst = block_rows(t)
            if relay_in:
                src = rows_in(t, dil, prev)
                q, k, v = qr_ref[src, :], kr_ref[src, :], vr_ref[src, :]
            else:
                src = rows_in(t, dil, 1)
                q, k, v = q_ref[0, src, :], k_ref[0, src, :], v_ref[0, src, :]
            if relay_out:
                qr_ref[dst, :], kr_ref[dst, :], vr_ref[dst, :] = q, k, v
            qd_ref[dst, :] = (q * (HEAD_DIM ** -0.5 * LOG2E)).astype(BF16)
            kd_ref[dst, :] = k.astype(BF16)
            vd_ref[dst, :] = v.astype(BF16)
            return carry

        lax.fori_loop(0, nsteps, gather, 0, unroll=unroll)

        def step(t, carry, pi=pi, dil=dil, prev=prev, sub_len=sub_len, nblk=nblk,
                 relay_in=relay_in, relay_out=relay_out):
            i = t % nblk
            kstart = jnp.clip(i * ATTN_QB - BAND, 0, sub_len - ATTN_KB)
            variant = jnp.where(i == 0, 0, jnp.where(i == nblk - 1, 2, 1))
            krows = pl.ds(pl.multiple_of((t - i) * ATTN_QB + kstart, BAND), ATTN_KB)
            q = qd_ref[block_rows(t), :]
            k = kd_ref[krows, :]
            v = vd_ref[krows, :]
            pv, den, top = [], [], []
            for h in range(2):
                qh = jnp.where(head0, q, zero) if h == 0 else jnp.where(head0, zero, q)
                s = lax.dot_general(qh, k, (((1,), (1,)), ((), ())), preferred_element_type=F32)
                s = s + bias_ref[pi, variant, h]
                m = jnp.max(s, axis=-1, keepdims=True)
                p = jnp.exp2(s - m)
                pv.append(jnp.dot(p.astype(BF16), v, preferred_element_type=F32))
                den.append(jnp.sum(p, axis=-1, keepdims=True))
                top.append(m)
            acc = jnp.where(head0, pv[0], pv[1])
            l = jnp.where(head0, den[0], den[1])
            m = jnp.where(head0, top[0], top[1])
            if pi > 0:
                if relay_in:
                    src = rows_in(t, dil, prev)
                    acc_run, m_run, l_run = ar_ref[src, :], mr_ref[src, :], lr_ref[src, :]
                else:
                    src = rows_in(t, dil, 1)
                    acc_run, m_run, l_run = o_ref[0, src, :], m_ref[src, :], l_ref[src, :]
                m_new = jnp.maximum(m, m_run)
                w_cur = jnp.exp2(m - m_new)
                w_run = jnp.exp2(m_run - m_new)
                acc = w_cur * acc + w_run * acc_run
                l = w_cur * l + w_run * l_run
                m = m_new
            if pi == last:
                o_ref[0, rows_in(t, dil, 1), :] = acc / l
            elif relay_out:
                dst = block_rows(t)
                ar_ref[dst, :], mr_ref[dst, :], lr_ref[dst, :] = acc, m, l
            else:
                dst = rows_in(t, dil, 1)
                o_ref[0, dst, :], m_ref[dst, :], l_ref[dst, :] = acc, m, l
            return carry

        lax.fori_loop(0, nsteps, step, 0, unroll=unroll)


def _attn(z3, bias, unroll=8):
    bsz, seq, _ = z3.shape
    assert all(seq // d >= 2 * ATTN_QB and seq % (d * ATTN_QB) == 0 for d in DILATIONS)
    assert DILATIONS[0] == 1 and len(DILATIONS) == 3 and DILATIONS[2] % DILATIONS[1] == 0
    col = lambda off: (lambda g, b: (b, 0, off // LANES + g))
    return pl.pallas_call(
        functools.partial(_attn_kernel, seq=seq, unroll=unroll),
        grid=(ATTN_WIDTH // LANES, bsz),
        in_specs=[pl.BlockSpec((1, seq, LANES), col(Q_OFF)),
                  pl.BlockSpec((1, seq, LANES), col(K_OFF)),
                  pl.BlockSpec((1, seq, LANES), col(V_OFF)),
                  pl.BlockSpec((len(DILATIONS), 3, LANES // HEAD_DIM, ATTN_QB, ATTN_KB),
                               lambda g, b: (0, 0, g, 0, 0))],
        out_specs=pl.BlockSpec((1, seq, LANES), lambda g, b: (b, 0, g)),
        out_shape=jax.ShapeDtypeStruct((bsz, seq, ATTN_WIDTH), F32),
        scratch_shapes=([pltpu.VMEM((seq, LANES), F32)] * 2 + [pltpu.VMEM((seq, LANES), BF16)] * 3
                        + [pltpu.VMEM((seq, LANES), F32)] * 6),
        compiler_params=_cparams(2, VMEM_LARGE_MIB),
        name="attn",
    )(z3, z3, z3, bias)


def _dft_mats():
    n = np.arange(FFT_N)
    ang = 2.0 * np.pi * np.outer(n, n) / FFT_N
    c, s = np.cos(ang), np.sin(ang)
    fa = np.concatenate([c, -s], axis=0)
    fb = np.concatenate([s, c], axis=0)
    ang_t = 2.0 * np.pi * np.outer(n, n) / (FFT_N * FFT_N)
    tw_c = np.broadcast_to(np.cos(ang_t)[:, :, None], (FFT_N, FFT_N, LANES))
    tw_s = np.broadcast_to(np.sin(ang_t)[:, :, None], (FFT_N, FFT_N, LANES))
    eye = np.eye(LANES // HEAD_DIM)
    scale = (FFT_N * FFT_N * HEAD_DIM) ** -0.5
    cc = np.kron(eye, c) * scale
    sc = np.kron(eye, s) * scale
    as_bf = lambda a: jnp.asarray(a, dtype=F32).astype(BF16)
    return (as_bf(fa), as_bf(fb), jnp.asarray(tw_c, F32), jnp.asarray(tw_s, F32), as_bf(cc), as_bf(sc))


def _fnet_kernel(x_ref, fa_ref, fb_ref, twc_ref, tws_ref, cc_ref, sc_ref, w_ref, o_ref,
                 bre_ref, bim_ref, zre_ref, zim_ref, *, epi_rows, unroll):
    def stage_a(s2, carry):
        x = x_ref[0, pl.ds(s2, FFT_N, stride=FFT_N), :].astype(BF16)
        a = jnp.dot(fa_ref[...], x, preferred_element_type=F32)
        ar, ai = a[:FFT_N], a[FFT_N:]
        c, s = twc_ref[s2], tws_ref[s2]
        dst = pl.ds(pl.multiple_of(s2 * FFT_N, FFT_N), FFT_N)
        bre_ref[dst, :] = ar * c + ai * s
        bim_ref[dst, :] = ai * c - ar * s
        return carry

    lax.fori_loop(0, FFT_N, stage_a, 0, unroll=unroll)

    def stage_b(k1, carry):
        src = pl.ds(k1, FFT_N, stride=FFT_N)
        z = (jnp.dot(fa_ref[...], bre_ref[src, :].astype(BF16), preferred_element_type=F32)
             + jnp.dot(fb_ref[...], bim_ref[src, :].astype(BF16), preferred_element_type=F32))
        zre_ref[src, :] = z[:FFT_N]
        zim_ref[src, :] = z[FFT_N:]
        return carry

    lax.fori_loop(0, FFT_N, stage_b, 0, unroll=unroll)

    for e in range(x_ref.shape[1] // epi_rows):
        rows = slice(e * epi_rows, (e + 1) * epi_rows)
        f = (jnp.dot(zre_ref[rows, :].astype(BF16), cc_ref[...], preferred_element_type=F32)
             + jnp.dot(zim_ref[rows, :].astype(BF16), sc_ref[...], preferred_element_type=F32))
        o_ref[0, rows, :] = jnp.dot(f.astype(BF16), w_ref[0], preferred_element_type=F32)


def _fnet(z3, mats, w_pairs):
    bsz, seq, _ = z3.shape
    assert seq == FFT_N * FFT_N
    fa, fb, tw_c, tw_s, cc, sc = mats
    const = lambda shape: pl.BlockSpec(shape, lambda b, j: (0,) * len(shape))
    return pl.pallas_call(
        functools.partial(_fnet_kernel, epi_rows=512, unroll=8),
        grid=(bsz, FNET_WIDTH // LANES),
        in_specs=[pl.BlockSpec((1, seq, LANES), lambda b, j: (b, 0, C_OFF // LANES + j)),
                  const((2 * FFT_N, FFT_N)), const((2 * FFT_N, FFT_N)),
                  const((FFT_N, FFT_N, LANES)), const((FFT_N, FFT_N, LANES)),
                  const((LANES, LANES)), const((LANES, LANES)),
                  pl.BlockSpec((1, LANES, LANES), lambda b, j: (j, 0, 0))],
        out_specs=pl.BlockSpec((1, seq, LANES), lambda b, j: (b, 0, j)),
        out_shape=jax.ShapeDtypeStruct((bsz, seq, FNET_WIDTH), F32),
        scratch_shapes=[pltpu.VMEM((seq, LANES), F32)] * 4,
        compiler_params=_cparams(2, VMEM_SMALL_MIB),
        name="fnet",
    )(z3, fa, fb, tw_c, tw_s, cc, sc, w_pairs)


def _out_proj_kernel(a_ref, b_ref, c_ref, g_ref, w_ref, x_ref, o_ref, mix_ref):
    o1, o2 = GMLP_WIDTH, GMLP_WIDTH + ATTN_WIDTH
    mix_ref[:, :o1] = _rms(a_ref[...].astype(F32), g_ref[:, :o1]).astype(BF16)
    mix_ref[:, o1:o2] = _rms(b_ref[...], g_ref[:, o1:o2]).astype(BF16)
    mix_ref[:, o2:] = _rms(c_ref[...], g_ref[:, o2:]).astype(BF16)
    o_ref[...] = x_ref[...] + jnp.dot(mix_ref[...], w_ref[...], preferred_element_type=F32)


def _out_proj(a_out, b_out, c_out, gain, w, layer, x2, tm=512):
    t = x2.shape[0]
    rows = lambda width: pl.BlockSpec((tm, width), lambda i: (i, 0))
    return pl.pallas_call(
        _out_proj_kernel,
        grid=(t // tm,),
        in_specs=[rows(GMLP_WIDTH), rows(ATTN_WIDTH), rows(FNET_WIDTH),
                  pl.BlockSpec((1, D_MODEL), lambda i: (0, 0)),
                  pl.BlockSpec((None, D_MODEL, D_MODEL), lambda i: (layer, 0, 0)),
                  rows(D_MODEL)],
        out_specs=rows(D_MODEL),
        out_shape=jax.ShapeDtypeStruct((t, D_MODEL), F32),
        scratch_shapes=[pltpu.VMEM((tm, D_MODEL), BF16)],
        compiler_params=_cparams(1, VMEM_LARGE_MIB),
        name="out_proj",
    )(a_out, b_out, c_out, gain, w, x2)


def _ffn_up_kernel(x_ref, xp_ref, xn_ref, g_ref, wg_ref, wu_ref, cwg_ref, cwu_ref, cbg_ref, cbu_ref,
                   o_ref, hn_ref, *, tiles_per_seq):
    i = pl.program_id(0)
    j = pl.program_id(1)
    tm = x_ref.shape[0]
    ext = tm + 2 * HALO

    @pl.when(j == 0)
    def _():
        has_prev = (i % tiles_per_seq != 0).astype(F32)
        has_next = (i % tiles_per_seq != tiles_per_seq - 1).astype(F32)
        hn_ref[:HALO] = (_rms(xp_ref[...], g_ref[...]) * has_prev).astype(BF16)
        hn_ref[HALO:HALO + tm] = _rms(x_ref[...], g_ref[...]).astype(BF16)
        hn_ref[HALO + tm:] = (_rms(xn_ref[...], g_ref[...]) * has_next).astype(BF16)

    def branch(w_ref, cw_ref, cb_ref):
        h = jnp.dot(hn_ref[...], w_ref[...], preferred_element_type=F32)
        up = pltpu.roll(h, 1, axis=0)[HALO:HALO + tm]
        dn = pltpu.roll(h, ext - 1, axis=0)[HALO:HALO + tm]
        return up * cw_ref[0:1] + h[HALO:HALO + tm] * cw_ref[1:2] + dn * cw_ref[2:3] + cb_ref[...]

    gate = branch(wg_ref, cwg_ref, cbg_ref)
    up = branch(wu_ref, cwu_ref, cbu_ref)
    o_ref[...] = (gate * jax.nn.sigmoid(gate) * up).astype(BF16)


def _ffn_up(x2, gain, w, layer, conv_w, conv_b, seq, tm=1024, tn=512):
    t = x2.shape[0]
    nj = D_FF // tn
    hb = tm // HALO
    last = t // HALO - 1
    return pl.pallas_call(
        functools.partial(_ffn_up_kernel, tiles_per_seq=seq // tm),
        grid=(t // tm, nj),
        in_specs=[pl.BlockSpec((tm, D_MODEL), lambda i, j: (i, 0)),
                  pl.BlockSpec((HALO, D_MODEL), lambda i, j: (jnp.maximum(i * hb - 1, 0), 0)),
                  pl.BlockSpec((HALO, D_MODEL), lambda i, j: (jnp.minimum((i + 1) * hb, last), 0)),
                  pl.BlockSpec((1, D_MODEL), lambda i, j: (0, 0)),
                  pl.BlockSpec((None, D_MODEL, tn), lambda i, j: (layer, 0, j)),
                  pl.BlockSpec((None, D_MODEL, tn), lambda i, j: (layer, 0, j + nj)),
                  pl.BlockSpec((3, tn), lambda i, j: (0, j)),
                  pl.BlockSpec((3, tn), lambda i, j: (0, j + nj)),
                  pl.BlockSpec((1, tn), lambda i, j: (0, j)),
                  pl.BlockSpec((1, tn), lambda i, j: (0, j + nj))],
        out_specs=pl.BlockSpec((tm, tn), lambda i, j: (i, j)),
        out_shape=jax.ShapeDtypeStruct((t, D_FF), BF16),
        scratch_shapes=[pltpu.VMEM((tm + 2 * HALO, D_MODEL), BF16)],
        compiler_params=_cparams(2, VMEM_LARGE_MIB),
        name="ffn_up",
    )(x2, x2, x2, gain, w, w, conv_w, conv_w, conv_b, conv_b)


def _ffn_down_kernel(a_ref, w_ref, x_ref, o_ref):
    o_ref[...] = x_ref[...] + jnp.dot(a_ref[...], w_ref[...], preferred_element_type=F32)


def _ffn_down(act, w, layer, x2, tm=1024, tn=512):
    t = x2.shape[0]
    return pl.pallas_call(
        _ffn_down_kernel,
        grid=(t // tm, D_MODEL // tn),
        in_specs=[pl.BlockSpec((tm, D_FF), lambda i, j: (i, 0)),
                  pl.BlockSpec((None, D_FF, tn), lambda i, j: (layer, 0, j)),
                  pl.BlockSpec((tm, tn), lambda i, j: (i, j))],
        out_specs=pl.BlockSpec((tm, tn), lambda i, j: (i, j)),
        out_shape=jax.ShapeDtypeStruct((t, D_MODEL), F32),
        compiler_params=_cparams(2, VMEM_LARGE_MIB),
        name="ffn_down",
    )(act, w, x2)


def _final_norm_kernel(x_ref, g_ref, o_ref):
    o_ref[...] = _rms(x_ref[...], g_ref[...])


def _final_norm(x2, gain, tm=512):
    t = x2.shape[0]
    return pl.pallas_call(
        _final_norm_kernel,
        grid=(t // tm,),
        in_specs=[pl.BlockSpec((tm, D_MODEL), lambda i: (i, 0)),
                  pl.BlockSpec((1, D_MODEL), lambda i: (0, 0))],
        out_specs=pl.BlockSpec((tm, D_MODEL), lambda i: (i, 0)),
        out_shape=jax.ShapeDtypeStruct((t, D_MODEL), F32),
        compiler_params=_cparams(1, VMEM_SMALL_MIB),
        name="final_norm",
    )(x2, gain)


def kernel(x, w_in, gmlp_ws, gmlp_b, fnet_w, mix_gain, w_out, norm_mix, norm_ffn, ffn_up, ffn_conv_w,
           ffn_conv_b, ffn_down, rel_bias, final_norm):
    bsz, seq, _ = x.shape
    depth = w_in.shape[0]
    t = bsz * seq
    x2 = x.reshape(t, D_MODEL)
    mats = _dft_mats()
    bias = _bias_tiles(rel_bias)
    eye = jnp.eye(LANES // HEAD_DIM, dtype=F32)
    w_in, w_out, ffn_up, ffn_down = (w.astype(BF16) for w in (w_in, w_out, ffn_up, ffn_down))
    for l in range(depth):
        z = _proj_in(x2, norm_mix[l][None], w_in, l)
        ws_pairs = gmlp_ws[l].astype(BF16).reshape(GMLP_HEADS // 2, 2 * CHUNK, CHUNK)
        b_tile = jnp.repeat(gmlp_b[l].T, HEAD_DIM, axis=1)
        a_out = _gmlp(z, ws_pairs, b_tile)
        z3 = z.reshape(bsz, seq, IN_WIDTH)
        b_out = _attn(z3, bias).reshape(t, ATTN_WIDTH)
        w_pairs = jnp.einsum("gh,pgce->pgche", eye, fnet_w[l].reshape(-1, 2, HEAD_DIM, HEAD_DIM))
        w_pairs = w_pairs.reshape(-1, LANES, LANES).astype(BF16)
        c_out = _fnet(z3, mats, w_pairs).reshape(t, FNET_WIDTH)
        x2 = _out_proj(a_out, b_out, c_out, mix_gain[l][None], w_out, l, x2)
        act = _ffn_up(x2, norm_ffn[l][None], ffn_up, l, ffn_conv_w[l], ffn_conv_b[l][None], seq)
        x2 = _ffn_down(act, ffn_down, l, x2)
    return _final_norm(x2, final_norm[None]).reshape(bsz, seq, D_MODEL)
```

```python
import functools

import numpy as np
import jax
import jax.numpy as jnp
from jax import lax
from jax.experimental import pallas as pl
from jax.experimental.pallas import tpu as pltpu

D_MODEL = 2048
HEAD_DIM = 64
GMLP_WIDTH = 512
ATTN_WIDTH = 1024
FNET_WIDTH = 512
GMLP_HEADS = GMLP_WIDTH // HEAD_DIM
ATTN_HEADS = ATTN_WIDTH // HEAD_DIM
FNET_GROUPS = FNET_WIDTH // HEAD_DIM
CHUNK = 128
DILATIONS = (1, 4, 16)
REL_BUCKETS = 32
REL_MAX_DISTANCE = 1024
D_FF = 5632
EPS = 1e-6
IN_WIDTH = 2 * GMLP_WIDTH + 3 * ATTN_WIDTH + FNET_WIDTH
NEG_INF = -1e30
LOG2E = 1.4426950408889634

Q_OFF = 2 * GMLP_WIDTH
K_OFF = Q_OFF + ATTN_WIDTH
V_OFF = K_OFF + ATTN_WIDTH
C_OFF = V_OFF + ATTN_WIDTH

LANES = 128
MXU_N = 256
BAND = 64
ATTN_QB = 128
ATTN_KB = ATTN_QB + 2 * BAND
FFT_N = 64
HALO = 16
BIAS_PAD = 512
BIAS_SHIFTS = (BIAS_PAD - BAND, 0, BAND)

F32 = jnp.float32
BF16 = jnp.bfloat16
MIB = 1024 * 1024
VMEM_LARGE_MIB = 56
VMEM_SMALL_MIB = 40


def _cparams(n_axes, vmem_mib):
    return pltpu.CompilerParams(dimension_semantics=("arbitrary",) * n_axes,
                                vmem_limit_bytes=vmem_mib * MIB)


def _rms(x, g):
    return x * lax.rsqrt(jnp.mean(x * x, axis=-1, keepdims=True) + EPS) * g


def _proj_in_kernel(x_ref, g_ref, w_ref, o_ref, xn_ref, *, gelu_cols):
    j = pl.program_id(1)
    tn = o_ref.shape[1]

    @pl.when(j == 0)
    def _():
        xn_ref[...] = _rms(x_ref[...], g_ref[...]).astype(BF16)

    o_ref[...] = jnp.dot(xn_ref[...], w_ref[...], preferred_element_type=F32)

    def gelu_in_place(cols):
        acc = o_ref[:, cols]
        o_ref[:, cols] = 0.5 * acc * (1.0 + lax.erf(acc * np.float32(2.0 ** -0.5)))

    if gelu_cols // tn:
        @pl.when(j < gelu_cols // tn)
        def _():
            gelu_in_place(slice(None))

    if gelu_cols % tn:
        @pl.when(j == gelu_cols // tn)
        def _():
            gelu_in_place(slice(0, gelu_cols % tn))


def _proj_in(x2, gain, w, layer, tm=1024, tn=1536):
    t = x2.shape[0]
    return pl.pallas_call(
        functools.partial(_proj_in_kernel, gelu_cols=Q_OFF),
        grid=(t // tm, IN_WIDTH // tn),
        in_specs=[pl.BlockSpec((tm, D_MODEL), lambda i, j: (i, 0)),
                  pl.BlockSpec((1, D_MODEL), lambda i, j: (0, 0)),
                  pl.BlockSpec((None, D_MODEL, tn), lambda i, j: (layer, 0, j))],
        out_specs=pl.BlockSpec((tm, tn), lambda i, j: (i, j)),
        out_shape=jax.ShapeDtypeStruct((t, IN_WIDTH), F32),
        scratch_shapes=[pltpu.VMEM((tm, D_MODEL), BF16)],
        compiler_params=_cparams(2, VMEM_LARGE_MIB),
        name="proj_in",
    )(x2, gain, w)


def _gmlp_kernel(u_ref, v_ref, ws_ref, b_ref, o_ref, *, chunks):
    lane = lax.broadcasted_iota(jnp.int32, (CHUNK, 2 * HEAD_DIM), 1)
    for c in range(chunks):
        rows = slice(c * CHUNK, (c + 1) * CHUNK)
        for p in range(GMLP_HEADS // 2):
            cols = slice(p * 2 * HEAD_DIM, (p + 1) * 2 * HEAD_DIM)
            both = jnp.dot(ws_ref[p], v_ref[rows, cols].astype(BF16), preferred_element_type=F32)
            gate = jnp.where(lane < HEAD_DIM, both[:CHUNK], both[CHUNK:]) + b_ref[:, cols]
            o_ref[rows, cols] = (u_ref[rows, cols] * gate).astype(BF16)


def _gmlp(z, ws_pairs, bias_tile, chunks=4):
    t = z.shape[0]
    tm = chunks * CHUNK
    return pl.pallas_call(
        functools.partial(_gmlp_kernel, chunks=chunks),
        grid=(t // tm,),
        in_specs=[pl.BlockSpec((tm, GMLP_WIDTH), lambda i: (i, 0)),
                  pl.BlockSpec((tm, GMLP_WIDTH), lambda i: (i, 1)),
                  pl.BlockSpec((GMLP_HEADS // 2, 2 * CHUNK, CHUNK), lambda i: (0, 0, 0)),
                  pl.BlockSpec((CHUNK, GMLP_WIDTH), lambda i: (0, 0))],
        out_specs=pl.BlockSpec((tm, GMLP_WIDTH), lambda i: (i, 0)),
        out_shape=jax.ShapeDtypeStruct((t, GMLP_WIDTH), BF16),
        compiler_params=_cparams(1, VMEM_SMALL_MIB),
        name="gmlp",
    )(z, z, ws_pairs, bias_tile)


def _t5_bucket(rel):
    half = REL_BUCKETS // 2
    max_exact = half // 2
    n = np.abs(rel)
    nl = np.maximum(n, max_exact).astype(np.float32)
    large = max_exact + (np.log(nl / max_exact) / np.log(REL_MAX_DISTANCE / max_exact)
                         * (half - max_exact)).astype(np.int32)
    large = np.minimum(large, half - 1)
    b = np.where(n < max_exact, n, large) + (rel > 0).astype(np.int32) * half
    return b.astype(np.int32)


def _bias_kernel(band_ref, o_ref):
    x = jnp.broadcast_to(band_ref[0, 0] * LOG2E, (ATTN_QB, BIAS_PAD))
    for variant, shift in enumerate(BIAS_SHIFTS):
        t = pltpu.roll(x, shift, 1, stride=1, stride_axis=0)
        o_ref[0, variant, 0] = t[:, :ATTN_KB]


def _bias_tiles(rel_bias):
    bands = []
    for dil in DILATIONS:
        offs = dil * np.arange(-BAND, BAND + 1, dtype=np.int32)
        bands.append(rel_bias[jnp.asarray(_t5_bucket(offs))].T.astype(F32))
    band = jnp.pad(jnp.stack(bands), ((0, 0), (0, 0), (0, BIAS_PAD - 2 * BAND - 1)), constant_values=NEG_INF)
    return pl.pallas_call(
        _bias_kernel,
        grid=(len(DILATIONS), ATTN_HEADS),
        in_specs=[pl.BlockSpec((1, 1, 1, BIAS_PAD), lambda p, h: (p, h, 0, 0))],
        out_specs=pl.BlockSpec((1, 3, 1, ATTN_QB, ATTN_KB), lambda p, h: (p, 0, h, 0, 0)),
        out_shape=jax.ShapeDtypeStruct((len(DILATIONS), 3, ATTN_HEADS, ATTN_QB, ATTN_KB), F32),
        compiler_params=_cparams(2, VMEM_SMALL_MIB),
        name="bias_tiles",
    )(band[:, :, None, :])


def _attn_kernel(q_ref, k_ref, v_ref, bias_ref, o_ref, m_ref, l_ref, qd_ref, kd_ref, vd_ref,
                 qr_ref, kr_ref, vr_ref, ar_ref, mr_ref, lr_ref, *, seq, unroll):
    head0 = lax.broadcasted_iota(jnp.int32, (ATTN_QB, LANES), 1) < HEAD_DIM
    zero = jnp.zeros((ATTN_QB, LANES), BF16)
    nsteps = seq // ATTN_QB
    last = len(DILATIONS) - 1

    def block_rows(t):
        return pl.ds(pl.multiple_of(t * ATTN_QB, ATTN_QB), ATTN_QB)

    def rows_in(t, dil, base):
        nblk = seq // dil // ATTN_QB
        r, i = t // nblk, t % nblk
        if dil == base:
            return block_rows(t)
        start = (r % base) * (seq // base) + r // base + (dil // base) * ATTN_QB * i
        return pl.ds(start, ATTN_QB, stride=dil // base)

    for pi, dil in enumerate(DILATIONS):
        sub_len = seq // dil
        nblk = sub_len // ATTN_QB
        prev = DILATIONS[pi - 1] if pi else 1
        relay_in = pi >= 2
        relay_out = 0 < pi < last

        def gather(t, carry, dil=dil, prev=prev, relay_in=relay_in, relay_out=relay_out):
            dst = block_rows(t)
            if relay_in:
                src = rows_in(t, dil, prev)
                q, k, v = qr_ref[src, :], kr_ref[src, :], vr_ref[src, :]
            else:
                src = rows_in(t, dil, 1)
                q, k, v = q_ref[0, src, :], k_ref[0, src, :], v_ref[0, src, :]
            if relay_out:
                qr_ref[dst, :], kr_ref[dst, :], vr_ref[dst, :] = q, k, v
            qd_ref[dst, :] = (q * (HEAD_DIM ** -0.5 * LOG2E)).astype(BF16)
            kd_ref[dst, :] = k.astype(BF16)
            vd_ref[dst, :] = v.astype(BF16)
            return carry

        lax.fori_loop(0, nsteps, gather, 0, unroll=unroll)

        def step(t, carry, pi=pi, dil=dil, prev=prev, sub_len=sub_len, nblk=nblk,
                 relay_in=relay_in, relay_out=relay_out):
            i = t % nblk
            kstart = jnp.clip(i * ATTN_QB - BAND, 0, sub_len - ATTN_KB)
            variant = jnp.where(i == 0, 0, jnp.where(i == nblk - 1, 2, 1))
            krows = pl.ds(pl.multiple_of((t - i) * ATTN_QB + kstart, BAND), ATTN_KB)
            q = qd_ref[block_rows(t), :]
            k = kd_ref[krows, :]
            v = vd_ref[krows, :]
            pv, den, top = [], [], []
            for h in range(2):
                qh = jnp.where(head0, q, zero) if h == 0 else jnp.where(head0, zero, q)
                s = lax.dot_general(qh, k, (((1,), (1,)), ((), ())), preferred_element_type=F32)
                s = s + bias_ref[pi, variant, h]
                m = jnp.max(s, axis=-1, keepdims=True)
                p = jnp.exp2(s - m)
                pv.append(jnp.dot(p.astype(BF16), v, preferred_element_type=F32))
                den.append(jnp.sum(p, axis=-1, keepdims=True))
                top.append(m)
            acc = jnp.where(head0, pv[0], pv[1])
            l = jnp.where(head0, den[0], den[1])
            m = jnp.where(head0, top[0], top[1])
            if pi > 0:
                if relay_in:
                    src = rows_in(t, dil, prev)
                    acc_run, m_run, l_run = ar_ref[src, :], mr_ref[src, :], lr_ref[src, :]
                else:
                    src = rows_in(t, dil, 1)
                    acc_run, m_run, l_run = o_ref[0, src, :], m_ref[src, :], l_ref[src, :]
                m_new = jnp.maximum(m, m_run)
                w_cur = jnp.exp2(m - m_new)
                w_run = jnp.exp2(m_run - m_new)
                acc = w_cur * acc + w_run * acc_run
                l = w_cur * l + w_run * l_run
                m = m_new
            if pi == last:
                o_ref[0, rows_in(t, dil, 1), :] = acc / l
            elif relay_out:
                dst = block_rows(t)
                ar_ref[dst, :], mr_ref[dst, :], lr_ref[dst, :] = acc, m, l
            else:
                dst = rows_in(t, dil, 1)
                o_ref[0, dst, :], m_ref[dst, :], l_ref[dst, :] = acc, m, l
            return carry

        lax.fori_loop(0, nsteps, step, 0, unroll=unroll)


def _attn(z3, bias, unroll=8):
    bsz, seq, _ = z3.shape
    assert all(seq // d >= 2 * ATTN_QB and seq % (d * ATTN_QB) == 0 for d in DILATIONS)
    assert DILATIONS[0] == 1 and len(DILATIONS) == 3 and DILATIONS[2] % DILATIONS[1] == 0
    col = lambda off: (lambda g, b: (b, 0, off // LANES + g))
    return pl.pallas_call(
        functools.partial(_attn_kernel, seq=seq, unroll=unroll),
        grid=(ATTN_WIDTH // LANES, bsz),
        in_specs=[pl.BlockSpec((1, seq, LANES), col(Q_OFF)),
                  pl.BlockSpec((1, seq, LANES), col(K_OFF)),
                  pl.BlockSpec((1, seq, LANES), col(V_OFF)),
                  pl.BlockSpec((len(DILATIONS), 3, LANES // HEAD_DIM, ATTN_QB, ATTN_KB),
                               lambda g, b: (0, 0, g, 0, 0))],
        out_specs=pl.BlockSpec((1, seq, LANES), lambda g, b: (b, 0, g)),
        out_shape=jax.ShapeDtypeStruct((bsz, seq, ATTN_WIDTH), F32),
        scratch_shapes=([pltpu.VMEM((seq, LANES), F32)] * 2 + [pltpu.VMEM((seq, LANES), BF16)] * 3
                        + [pltpu.VMEM((seq, LANES), F32)] * 6),
        compiler_params=_cparams(2, VMEM_LARGE_MIB),
        name="attn",
    )(z3, z3, z3, bias)


def _dft_mats():
    n = np.arange(FFT_N)
    ang = 2.0 * np.pi * np.outer(n, n) / FFT_N
    c, s = np.cos(ang), np.sin(ang)
    fa = np.concatenate([c, -s], axis=0)
    fb = np.concatenate([s, c], axis=0)
    ang_t = 2.0 * np.pi * np.outer(n, n) / (FFT_N * FFT_N)
    tw_c = np.broadcast_to(np.cos(ang_t)[:, :, None], (FFT_N, FFT_N, LANES))
    tw_s = np.broadcast_to(np.sin(ang_t)[:, :, None], (FFT_N, FFT_N, LANES))
    eye = np.eye(LANES // HEAD_DIM)
    scale = (FFT_N * FFT_N * HEAD_DIM) ** -0.5
    cc = np.kron(eye, c) * scale
    sc = np.kron(eye, s) * scale
    as_bf = lambda a: jnp.asarray(a, dtype=F32).astype(BF16)
    return (as_bf(fa), as_bf(fb), jnp.asarray(tw_c, F32), jnp.asarray(tw_s, F32), as_bf(cc), as_bf(sc))


def _fnet_kernel(x_ref, fa_ref, fb_ref, twc_ref, tws_ref, cc_ref, sc_ref, w_ref, o_ref,
                 bre_ref, bim_ref, zre_ref, zim_ref, *, epi_rows, unroll):
    def stage_a(s2, carry):
        x = x_ref[0, pl.ds(s2, FFT_N, stride=FFT_N), :].astype(BF16)
        a = jnp.dot(fa_ref[...], x, preferred_element_type=F32)
        ar, ai = a[:FFT_N], a[FFT_N:]
        c, s = twc_ref[s2], tws_ref[s2]
        dst = pl.ds(pl.multiple_of(s2 * FFT_N, FFT_N), FFT_N)
        bre_ref[dst, :] = ar * c + ai * s
        bim_ref[dst, :] = ai * c - ar * s
        return carry

    lax.fori_loop(0, FFT_N, stage_a, 0, unroll=unroll)

    def stage_b(k1, carry):
        src = pl.ds(k1, FFT_N, stride=FFT_N)
        z = (jnp.dot(fa_ref[...], bre_ref[src, :].astype(BF16), preferred_element_type=F32)
             + jnp.dot(fb_ref[...], bim_ref[src, :].astype(BF16), preferred_element_type=F32))
        zre_ref[src, :] = z[:FFT_N]
        zim_ref[src, :] = z[FFT_N:]
        return carry

    lax.fori_loop(0, FFT_N, stage_b, 0, unroll=unroll)

    for e in range(x_ref.shape[1] // epi_rows):
        rows = slice(e * epi_rows, (e + 1) * epi_rows)
        f = (jnp.dot(zre_ref[rows, :].astype(BF16), cc_ref[...], preferred_element_type=F32)
             + jnp.dot(zim_ref[rows, :].astype(BF16), sc_ref[...], preferred_element_type=F32))
        o_ref[0, rows, :] = jnp.dot(f.astype(BF16), w_ref[0], preferred_element_type=F32)


def _fnet(z3, mats, w_pairs):
    bsz, seq, _ = z3.shape
    assert seq == FFT_N * FFT_N
    fa, fb, tw_c, tw_s, cc, sc = mats
    const = lambda shape: pl.BlockSpec(shape, lambda b, j: (0,) * len(shape))
    return pl.pallas_call(
        functools.partial(_fnet_kernel, epi_rows=512, unroll=8),
        grid=(bsz, FNET_WIDTH // LANES),
        in_specs=[pl.BlockSpec((1, seq, LANES), lambda b, j: (b, 0, C_OFF // LANES + j)),
                  const((2 * FFT_N, FFT_N)), const((2 * FFT_N, FFT_N)),
                  const((FFT_N, FFT_N, LANES)), const((FFT_N, FFT_N, LANES)),
                  const((LANES, LANES)), const((LANES, LANES)),
                  pl.BlockSpec((1, LANES, LANES), lambda b, j: (j, 0, 0))],
        out_specs=pl.BlockSpec((1, seq, LANES), lambda b, j: (b, 0, j)),
        out_shape=jax.ShapeDtypeStruct((bsz, seq, FNET_WIDTH), F32),
        scratch_shapes=[pltpu.VMEM((seq, LANES), F32)] * 4,
        compiler_params=_cparams(2, VMEM_SMALL_MIB),
        name="fnet",
    )(z3, fa, fb, tw_c, tw_s, cc, sc, w_pairs)


def _out_proj_kernel(a_ref, b_ref, c_ref, g_ref, w_ref, x_ref, o_ref, mix_ref):
    o1, o2 = GMLP_WIDTH, GMLP_WIDTH + ATTN_WIDTH
    mix_ref[:, :o1] = _rms(a_ref[...].astype(F32), g_ref[:, :o1]).astype(BF16)
    mix_ref[:, o1:o2] = _rms(b_ref[...], g_ref[:, o1:o2]).astype(BF16)
    mix_ref[:, o2:] = _rms(c_ref[...], g_ref[:, o2:]).astype(BF16)
    o_ref[...] = x_ref[...] + jnp.dot(mix_ref[...], w_ref[...], preferred_element_type=F32)


def _out_proj(a_out, b_out, c_out, gain, w, layer, x2, tm=512):
    t = x2.shape[0]
    rows = lambda width: pl.BlockSpec((tm, width), lambda i: (i, 0))
    return pl.pallas_call(
        _out_proj_kernel,
        grid=(t // tm,),
        in_specs=[rows(GMLP_WIDTH), rows(ATTN_WIDTH), rows(FNET_WIDTH),
                  pl.BlockSpec((1, D_MODEL), lambda i: (0, 0)),
                  pl.BlockSpec((None, D_MODEL, D_MODEL), lambda i: (layer, 0, 0)),
                  rows(D_MODEL)],
        out_specs=rows(D_MODEL),
        out_shape=jax.ShapeDtypeStruct((t, D_MODEL), F32),
        scratch_shapes=[pltpu.VMEM((tm, D_MODEL), BF16)],
        compiler_params=_cparams(1, VMEM_LARGE_MIB),
        name="out_proj",
    )(a_out, b_out, c_out, gain, w, x2)


def _ffn_up_kernel(x_ref, xp_ref, xn_ref, g_ref, wg_ref, wu_ref, cwg_ref, cwu_ref, cbg_ref, cbu_ref,
                   o_ref, hn_ref, *, tiles_per_seq):
    i = pl.program_id(0)
    j = pl.program_id(1)
    tm = x_ref.shape[0]
    ext = tm + 2 * HALO

    @pl.when(j == 0)
    def _():
        has_prev = (i % tiles_per_seq != 0).astype(F32)
        has_next = (i % tiles_per_seq != tiles_per_seq - 1).astype(F32)
        hn_ref[:HALO] = (_rms(xp_ref[...], g_ref[...]) * has_prev).astype(BF16)
        hn_ref[HALO:HALO + tm] = _rms(x_ref[...], g_ref[...]).astype(BF16)
        hn_ref[HALO + tm:] = (_rms(xn_ref[...], g_ref[...]) * has_next).astype(BF16)

    def branch(w_ref, cw_ref, cb_ref):
        h = jnp.dot(hn_ref[...], w_ref[...], preferred_element_type=F32)
        up = pltpu.roll(h, 1, axis=0)[HALO:HALO + tm]
        dn = pltpu.roll(h, ext - 1, axis=0)[HALO:HALO + tm]
        return up * cw_ref[0:1] + h[HALO:HALO + tm] * cw_ref[1:2] + dn * cw_ref[2:3] + cb_ref[...]

    gate = branch(wg_ref, cwg_ref, cbg_ref)
    up = branch(wu_ref, cwu_ref, cbu_ref)
    o_ref[...] = (gate * jax.nn.sigmoid(gate) * up).astype(BF16)


def _ffn_up(x2, gain, w, layer, conv_w, conv_b, seq, tm=1024, tn=512):
    t = x2.shape[0]
    nj = D_FF // tn
    hb = tm // HALO
    last = t // HALO - 1
    return pl.pallas_call(
        functools.partial(_ffn_up_kernel, tiles_per_seq=seq // tm),
        grid=(t // tm, nj),
        in_specs=[pl.BlockSpec((tm, D_MODEL), lambda i, j: (i, 0)),
                  pl.BlockSpec((HALO, D_MODEL), lambda i, j: (jnp.maximum(i * hb - 1, 0), 0)),
                  pl.BlockSpec((HALO, D_MODEL), lambda i, j: (jnp.minimum((i + 1) * hb, last), 0)),
                  pl.BlockSpec((1, D_MODEL), lambda i, j: (0, 0)),
                  pl.BlockSpec((None, D_MODEL, tn), lambda i, j: (layer, 0, j)),
                  pl.BlockSpec((None, D_MODEL, tn), lambda i, j: (layer, 0, j + nj)),
                  pl.BlockSpec((3, tn), lambda i, j: (0, j)),
                  pl.BlockSpec((3, tn), lambda i, j: (0, j + nj)),
                  pl.BlockSpec((1, tn), lambda i, j: (0, j)),
                  pl.BlockSpec((1, tn), lambda i, j: (0, j + nj))],
        out_specs=pl.BlockSpec((tm, tn), lambda i, j: (i, j)),
        out_shape=jax.ShapeDtypeStruct((t, D_FF), BF16),
        scratch_shapes=[pltpu.VMEM((tm + 2 * HALO, D_MODEL), BF16)],
        compiler_params=_cparams(2, VMEM_LARGE_MIB),
        name="ffn_up",
    )(x2, x2, x2, gain, w, w, conv_w, conv_w, conv_b, conv_b)


def _ffn_down_kernel(a_ref, w_ref, x_ref, o_ref):
    o_ref[...] = x_ref[...] + jnp.dot(a_ref[...], w_ref[...], preferred_element_type=F32)


def _ffn_down(act, w, layer, x2, tm=1024, tn=512):
    t = x2.shape[0]
    return pl.pallas_call(
        _ffn_down_kernel,
        grid=(t // tm, D_MODEL // tn),
        in_specs=[pl.BlockSpec((tm, D_FF), lambda i, j: (i, 0)),
                  pl.BlockSpec((None, D_FF, tn), lambda i, j: (layer, 0, j)),
                  pl.BlockSpec((tm, tn), lambda i, j: (i, j))],
        out_specs=pl.BlockSpec((tm, tn), lambda i, j: (i, j)),
        out_shape=jax.ShapeDtypeStruct((t, D_MODEL), F32),
        compiler_params=_cparams(2, VMEM_LARGE_MIB),
        name="ffn_down",
    )(act, w, x2)


def _final_norm_kernel(x_ref, g_ref, o_ref):
    o_ref[...] = _rms(x_ref[...], g_ref[...])


def _final_norm(x2, gain, tm=512):
    t = x2.shape[0]
    return pl.pallas_call(
        _final_norm_kernel,
        grid=(t // tm,),
        in_specs=[pl.BlockSpec((tm, D_MODEL), lambda i: (i, 0)),
                  pl.BlockSpec((1, D_MODEL), lambda i: (0, 0))],
        out_specs=pl.BlockSpec((tm, D_MODEL), lambda i: (i, 0)),
        out_shape=jax.ShapeDtypeStruct((t, D_MODEL), F32),
        compiler_params=_cparams(1, VMEM_SMALL_MIB),
        name="final_norm",
    )(x2, gain)


def kernel(x, w_in, gmlp_ws, gmlp_b, fnet_w, mix_gain, w_out, norm_mix, norm_ffn, ffn_up, ffn_conv_w,
           ffn_conv_b, ffn_down, rel_bias, final_norm):
    bsz, seq, _ = x.shape
    depth = w_in.shape[0]
    t = bsz * seq
    x2 = x.reshape(t, D_MODEL)
    mats = _dft_mats()
    bias = _bias_tiles(rel_bias)
    eye = jnp.eye(LANES // HEAD_DIM, dtype=F32)
    w_in, w_out, ffn_up, ffn_down = (w.astype(BF16) for w in (w_in, w_out, ffn_up, ffn_down))
    for l in range(depth):
        z = _proj_in(x2, norm_mix[l][None], w_in, l)
        ws_pairs = gmlp_ws[l].astype(BF16).reshape(GMLP_HEADS // 2, 2 * CHUNK, CHUNK)
        b_tile = jnp.repeat(gmlp_b[l].T, HEAD_DIM, axis=1)
        a_out = _gmlp(z, ws_pairs, b_tile)
        z3 = z.reshape(bsz, seq, IN_WIDTH)
        b_out = _attn(z3, bias).reshape(t, ATTN_WIDTH)
        w_pairs = jnp.einsum("gh,pgce->pgche", eye, fnet_w[l].reshape(-1, 2, HEAD_DIM, HEAD_DIM))
        w_pairs = w_pairs.reshape(-1, LANES, LANES).astype(BF16)
        c_out = _fnet(z3, mats, w_pairs).reshape(t, FNET_WIDTH)
        x2 = _out_proj(a_out, b_out, c_out, mix_gain[l][None], w_out, l, x2)
        act = _ffn_up(x2, norm_ffn[l][None], ffn_up, l, ffn_conv_w[l], ffn_conv_b[l][None], seq)
        x2 = _ffn_down(act, ffn_down, l, x2)
    return _final_norm(x2, final_norm[None]).reshape(bsz, seq, D_MODEL)
```
